```python
import math
import jax, jax.numpy as jnp
from jax import lax
import numpy as np

D_MODEL = 1024
BATCH = 16
SEQ = 2048
DEPTH = 1
DEC_BATCH = 32
DEC_SEQ = 4
PAST_LEN = 16384
PAGE_SIZE = 128

N_HEADS = 8
HEAD_DIM = 64
V_DIM = 2 * HEAD_DIM
QK_WIDTH = N_HEADS * 2 * HEAD_DIM
V_WIDTH = N_HEADS * V_DIM
NUM_BUCKETS = 32
MAX_DISTANCE = 128
Q_BLOCK = 128
POOL_WINDOWS = (2, 4, 8, 16)
N_POOL_GROUPS = 4
POOL_GROUP_DIM = 128
POOL_WIDTH = N_POOL_GROUPS * POOL_GROUP_DIM
POOL_CTX = max(POOL_WINDOWS) - 1
IN_WIDTH = 2 * QK_WIDTH + V_WIDTH + POOL_WIDTH
N_MEM = 256
X_HEADS = 4
X_HEAD_DIM = D_MODEL // X_HEADS
N_GROUPS = 4
EXPERTS_PER_GROUP = 8
N_EXPERTS = N_GROUPS * EXPERTS_PER_GROUP
TOP_K_IN_GROUP = 2
D_EXPERT = D_MODEL // 4
RMS_EPS = 1e-6

kernel_name = 'hybrid_diffattn_pool_hmoe_step'


def rmsnorm(x, g):
    xf = x.astype(jnp.float32)
    y = xf * lax.rsqrt(jnp.mean(xf * xf, axis=-1, keepdims=True) + RMS_EPS)
    return (y * g.astype(jnp.float32)).astype(x.dtype)


def t5_bucket(rel):
    n = jnp.maximum(-rel, 0)
    max_exact = NUM_BUCKETS // 2
    nf = jnp.maximum(n, max_exact).astype(jnp.float32)
    large = max_exact + (jnp.log(nf / max_exact) / math.log(MAX_DISTANCE / max_exact)
                         * (NUM_BUCKETS - max_exact)).astype(jnp.int32)
    large = jnp.minimum(large, NUM_BUCKETS - 1)
    return jnp.where(n < max_exact, n, large)


def diff_lambda(lq1, lk1, lq2, lk2, lam_init):
    f32 = jnp.float32
    return (jnp.exp(jnp.sum(lq1.astype(f32) * lk1.astype(f32)))
            - jnp.exp(jnp.sum(lq2.astype(f32) * lk2.astype(f32))) + lam_init)


def diff_attn_core(q, k, v, qpos, kpos, rel_bias, lam):
    b, nk = k.shape[:2]
    k2 = k.reshape(b, nk, N_HEADS, 2, HEAD_DIM)
    s = jnp.einsum('bqhmd,bkhmd->bmhqk', q, k2).astype(jnp.float32) * (HEAD_DIM ** -0.5)
    rel = kpos[None, :] - qpos[:, None]
    bias = jnp.transpose(rel_bias[t5_bucket(rel)], (2, 0, 1)).astype(jnp.float32)
    s = jnp.where(rel <= 0, s + bias, -jnp.inf)
    p = jax.nn.softmax(s, axis=-1)
    a = p[:, 0] - lam * p[:, 1]
    return jnp.einsum('bhqk,bkhd->bqhd', a.astype(v.dtype), v)


def prompt_attention(q, k, v, rel_bias, lam):
    b, s = q.shape[:2]
    nblk = s // Q_BLOCK
    qb = q.reshape(b, nblk, Q_BLOCK, N_HEADS, 2, HEAD_DIM).swapaxes(0, 1)
    posb = jnp.arange(s, dtype=jnp.int32).reshape(nblk, Q_BLOCK)
    kpos = jnp.arange(s, dtype=jnp.int32)
    out = lax.map(lambda a: diff_attn_core(a[0], k, v, a[1], kpos, rel_bias, lam), (qb, posb))
    return out.swapaxes(0, 1).reshape(b, s, N_HEADS, V_DIM)


def pool_mix(ctx, u, w_grp, scale):
    b, p = ctx.shape[:2]
    s = u.shape[1]
    full = jnp.concatenate([ctx, u], axis=1).astype(jnp.float32)
    c = jnp.concatenate([jnp.zeros((b, 1, POOL_WIDTH), jnp.float32), jnp.cumsum(full, axis=1)], axis=1)
    end = p + 1 + jnp.arange(s, dtype=jnp.int32)
    outs = []
    for g, w in enumerate(POOL_WINDOWS):
        lo = jnp.maximum(end - w, 0)
        sl = slice(g * POOL_GROUP_DIM, (g + 1) * POOL_GROUP_DIM)
        cg = c[:, :, sl]
        mean = (jnp.take(cg, end, axis=1) - jnp.take(cg, lo, axis=1)) / (end - lo).astype(jnp.float32)[None, :, None]
        outs.append(mean - u[:, :, sl].astype(jnp.float32))
    pooled = jnp.stack(outs, axis=2)
    mixed = jnp.einsum('bsgc,gcd->bsgd', pooled, w_grp.astype(jnp.float32)).reshape(b, s, POOL_WIDTH)
    return (mixed * scale.astype(jnp.float32)).astype(u.dtype)


def mixer_project(x, g, w_in):
    b, s = x.shape[:2]
    h = rmsnorm(x, g)
    proj = h @ w_in
    q = proj[..., :QK_WIDTH].reshape(b, s, N_HEADS, 2, HEAD_DIM)
    k = proj[..., QK_WIDTH:2 * QK_WIDTH].reshape(b, s, N_HEADS, V_DIM)
    v = proj[..., 2 * QK_WIDTH:2 * QK_WIDTH + V_WIDTH].reshape(b, s, N_HEADS, V_DIM)
    u = proj[..., 2 * QK_WIDTH + V_WIDTH:]
    return h, q, k, v, u


def mixer_merge(h, attn, pooled, lam_init, subln_g, w_br_attn, w_br_pool, w_gate, w_out):
    b, s = h.shape[:2]
    a = (rmsnorm(attn, subln_g) * (1.0 - lam_init)).reshape(b, s, V_WIDTH)
    gates = jax.nn.sigmoid(h @ w_gate)
    ga, gp = jnp.split(gates, 2, axis=-1)
    merged = ga * (a @ w_br_attn) + gp * (pooled @ w_br_pool)
    return merged @ w_out


def cross_attention(x, g, w_cq, mem_k, mem_v, w_co):
    b, s = x.shape[:2]
    h = rmsnorm(x, g)
    q = (h @ w_cq).reshape(b, s, X_HEADS, X_HEAD_DIM)
    sc = jnp.einsum('bqhd,bmhd->bhqm', q, mem_k).astype(jnp.float32) * (X_HEAD_DIM ** -0.5)
    p = jax.nn.softmax(sc, axis=-1)
    o = jnp.einsum('bhqm,bmhd->bqhd', p.astype(mem_v.dtype), mem_v).reshape(b, s, X_HEADS * X_HEAD_DIM)
    return o @ w_co


def moe_ffn(x, g, w_rg, b_rg, w_re, b_re, w1, w3, w2):
    h = rmsnorm(x, g)

    def per_seq(t):
        glog = (t @ w_rg).astype(jnp.float32) + b_rg.astype(jnp.float32)
        gprob = jax.nn.softmax(glog, axis=-1)
        grp = jnp.argmax(glog, axis=-1)
        p_grp = jnp.take_along_axis(gprob, grp[:, None], axis=-1)
        elog = ((t @ w_re).astype(jnp.float32) + b_re.astype(jnp.float32)).reshape(-1, N_GROUPS, EXPERTS_PER_GROUP)
        elog_g = jnp.take_along_axis(elog, grp[:, None, None], axis=1)[:, 0]
        top_v, top_i = lax.top_k(elog_g, TOP_K_IN_GROUP)
        wts = jax.nn.softmax(top_v, axis=-1) * p_grp
        eid = grp[:, None] * EXPERTS_PER_GROUP + top_i
        gates = jnp.sum(jax.nn.one_hot(eid, N_EXPERTS, dtype=jnp.float32) * wts[..., None], axis=1)
        h1 = jnp.einsum('td,edf->tef', t, w1)
        h3 = jnp.einsum('td,edf->tef', t, w3)
        act = jax.nn.silu(h1) * h3 * gates[..., None].astype(t.dtype)
        return jnp.einsum('tef,efd->td', act, w2)

    return lax.map(per_seq, h)


def setup_inputs(seed: int = 0) -> dict:
    key = jax.random.key(seed)
    ks = iter(jax.random.split(key, 48))

    def nrm(shape, scale=1.0):
        return jax.random.normal(next(ks), shape, jnp.float32) * scale

    def gain(shape):
        return 1.0 + nrm(shape, 0.01)

    L = DEPTH
    n_pages = PAST_LEN // PAGE_SIZE
    n_phys = (DEC_BATCH * n_pages * 5) // 4
    page_table = jax.random.permutation(next(ks), n_phys)[:DEC_BATCH * n_pages].reshape(DEC_BATCH, n_pages).astype(jnp.int32)
    return {
        'x_prompt': nrm((BATCH, SEQ, D_MODEL)),
        'x_sample': nrm((DEC_BATCH, DEC_SEQ, D_MODEL)),
        'mem_prompt': nrm((BATCH, N_MEM, D_MODEL)),
        'cache_k': nrm((L, n_phys, PAGE_SIZE, N_HEADS, V_DIM)),
        'cache_v': nrm((L, n_phys, PAGE_SIZE, N_HEADS, V_DIM)),
        'page_table': page_table,
        'cache_mem_k': nrm((L, DEC_BATCH, N_MEM, X_HEADS, X_HEAD_DIM)),
        'cache_mem_v': nrm((L, DEC_BATCH, N_MEM, X_HEADS, X_HEAD_DIM)),
        'state_pool': nrm((L, DEC_BATCH, POOL_CTX, POOL_WIDTH)),
        'rel_bias': nrm((NUM_BUCKETS, N_HEADS), 0.5),
        'norm_mix_g': gain((L, D_MODEL)),
        'w_in': nrm((L, D_MODEL, IN_WIDTH), D_MODEL ** -0.5),
        'lambda_q1': nrm((L, HEAD_DIM), 0.1),
        'lambda_k1': nrm((L, HEAD_DIM), 0.1),
        'lambda_q2': nrm((L, HEAD_DIM), 0.1),
        'lambda_k2': nrm((L, HEAD_DIM), 0.1),
        'subln_g': gain((L, V_DIM)),
        'w_pool_grp': nrm((L, N_POOL_GROUPS, POOL_GROUP_DIM, POOL_GROUP_DIM), POOL_GROUP_DIM ** -0.5),
        'pool_scale': 1.0 + nrm((L, POOL_WIDTH), 0.1),
        'w_br_attn': nrm((L, V_WIDTH, D_MODEL), V_WIDTH ** -0.5),
        'w_br_pool': nrm((L, POOL_WIDTH, D_MODEL), POOL_WIDTH ** -0.5),
        'w_gate': nrm((L, D_MODEL, 2 * D_MODEL), D_MODEL ** -0.5),
        'w_out': nrm((L, D_MODEL, D_MODEL), D_MODEL ** -0.5),
        'norm_cross_g': gain((L, D_MODEL)),
        'w_cq': nrm((L, D_MODEL, X_HEADS * X_HEAD_DIM), D_MODEL ** -0.5),
        'w_ck': nrm((L, D_MODEL, X_HEADS * X_HEAD_DIM), D_MODEL ** -0.5),
        'w_cv': nrm((L, D_MODEL, X_HEADS * X_HEAD_DIM), D_MODEL ** -0.5),
        'w_co': nrm((L, X_HEADS * X_HEAD_DIM, D_MODEL), D_MODEL ** -0.5),
        'norm_ffn_g': gain((L, D_MODEL)),
        'w_router_grp': nrm((L, D_MODEL, N_GROUPS), D_MODEL ** -0.5),
        'b_router_grp': nrm((L, N_GROUPS), 0.01),
        'w_router_exp': nrm((L, D_MODEL, N_EXPERTS), D_MODEL ** -0.5),
        'b_router_exp': nrm((L, N_EXPERTS), 0.01),
        'w_exp_gate': nrm((L, N_EXPERTS, D_MODEL, D_EXPERT), D_MODEL ** -0.5),
        'w_exp_up': nrm((L, N_EXPERTS, D_MODEL, D_EXPERT), D_MODEL ** -0.5),
        'w_exp_down': nrm((L, N_EXPERTS, D_EXPERT, D_MODEL), D_EXPERT ** -0.5),
        'norm_final_g': gain((D_MODEL,)),
    }


def reference(x_prompt, x_sample, mem_prompt, cache_k, cache_v, page_table, cache_mem_k, cache_mem_v,
              state_pool, rel_bias, norm_mix_g, w_in, lambda_q1, lambda_k1, lambda_q2, lambda_k2, subln_g,
              w_pool_grp, pool_scale, w_br_attn, w_br_pool, w_gate, w_out, norm_cross_g, w_cq, w_ck, w_cv,
              w_co, norm_ffn_g, w_router_grp, b_router_grp, w_router_exp, b_router_exp, w_exp_gate, w_exp_up,
              w_exp_down, norm_final_g):
    xp, xs = x_prompt, x_sample
    b_p = xp.shape[0]
    b_s, s_s = xs.shape[:2]
    past_len = page_table.shape[1] * cache_k.shape[2]
    kp_l, vp_l, pp_l, mkp_l, mvp_l, ks_l, vs_l, ps_l = [], [], [], [], [], [], [], []
    for l in range(DEPTH):
        lam_init = 0.8 - 0.6 * math.exp(-0.3 * l)
        lam = diff_lambda(lambda_q1[l], lambda_k1[l], lambda_q2[l], lambda_k2[l], lam_init)
        merge_w = (subln_g[l], w_br_attn[l], w_br_pool[l], w_gate[l], w_out[l])
        moe_w = (w_router_grp[l], b_router_grp[l], w_router_exp[l], b_router_exp[l],
                 w_exp_gate[l], w_exp_up[l], w_exp_down[l])

        h, q, k, v, u = mixer_project(xp, norm_mix_g[l], w_in[l])
        attn = prompt_attention(q, k, v, rel_bias, lam)
        pooled = pool_mix(u[:, :0], u, w_pool_grp[l], pool_scale[l])
        xp = xp + mixer_merge(h, attn, pooled, lam_init, *merge_w)
        mem_k = (mem_prompt @ w_ck[l]).reshape(b_p, -1, X_HEADS, X_HEAD_DIM)
        mem_v = (mem_prompt @ w_cv[l]).reshape(b_p, -1, X_HEADS, X_HEAD_DIM)
        xp = xp + cross_attention(xp, norm_cross_g[l], w_cq[l], mem_k, mem_v, w_co[l])
        xp = xp + moe_ffn(xp, norm_ffn_g[l], *moe_w)
        kp_l.append(k)
        vp_l.append(v)
        pp_l.append(u[:, -POOL_CTX:])
        mkp_l.append(mem_k)
        mvp_l.append(mem_v)

        h, q, k, v, u = mixer_project(xs, norm_mix_g[l], w_in[l])
        past_k = cache_k[l, page_table].reshape(b_s, past_len, N_HEADS, V_DIM)
        past_v = cache_v[l, page_table].reshape(b_s, past_len, N_HEADS, V_DIM)
        k_all = jnp.concatenate([past_k, k.astype(past_k.dtype)], axis=1)
        v_all = jnp.concatenate([past_v, v.astype(past_v.dtype)], axis=1)
        qpos = past_len + jnp.arange(s_s, dtype=jnp.int32)
        kpos = jnp.arange(past_len + s_s, dtype=jnp.int32)
        attn = diff_attn_core(q, k_all, v_all, qpos, kpos, rel_bias, lam)
        ctx = state_pool[l].astype(u.dtype)
        pooled = pool_mix(ctx, u, w_pool_grp[l], pool_scale[l])
        xs = xs + mixer_merge(h, attn, pooled, lam_init, *merge_w)
        xs = xs + cross_attention(xs, norm_cross_g[l], w_cq[l], cache_mem_k[l], cache_mem_v[l], w_co[l])
        xs = xs + moe_ffn(xs, norm_ffn_g[l], *moe_w)
        ks_l.append(k)
        vs_l.append(v)
        ps_l.append(jnp.concatenate([ctx, u], axis=1)[:, -POOL_CTX:])

    y_prompt = rmsnorm(xp, norm_final_g)
    y_sample = rmsnorm(xs, norm_final_g)
    k_prompt = jnp.stack(kp_l)
    v_prompt = jnp.stack(vp_l)
    pool_prompt = jnp.stack(pp_l)
    mem_k_prompt = jnp.stack(mkp_l)
    mem_v_prompt = jnp.stack(mvp_l)
    k_sample = jnp.stack(ks_l)
    v_sample = jnp.stack(vs_l)
    pool_sample = jnp.stack(ps_l)
    return (y_prompt, y_sample, k_prompt, v_prompt, pool_prompt, mem_k_prompt, mem_v_prompt, k_sample, v_sample, pool_sample)
```

```python
import functools
import math

import jax
import jax.numpy as jnp
from jax import lax
from jax.experimental import pallas as pl
from jax.experimental.pallas import tpu as pltpu

F32 = jnp.float32
BF16 = jnp.bfloat16

RMS_EPS = 1e-6
N_HEADS = 8
HEAD_DIM = 64
V_DIM = 2 * HEAD_DIM
NUM_BUCKETS = 32
MAX_DISTANCE = 128
POOL_WINDOWS = (2, 4, 8, 16)
POOL_GROUP_DIM = 128
POOL_CTX = max(POOL_WINDOWS) - 1
X_HEADS = 4
N_GROUPS = 4
EXPERTS_PER_GROUP = 8
N_EXPERTS = N_GROUPS * EXPERTS_PER_GROUP
LANES = 128
BF16_ROWS = 16
VMEM_LIMIT = 56 * 1024 * 1024
PAGES_PER_STEP = 4
NEG_INF = float("-inf")


def _cparams(*sem):
    return pltpu.CompilerParams(dimension_semantics=sem, vmem_limit_bytes=VMEM_LIMIT)


def _rms(x, g):
    return x * lax.rsqrt(jnp.mean(x * x, axis=-1, keepdims=True) + RMS_EPS) * g


def _dot(a, b):
    return jnp.dot(a, b, preferred_element_type=F32)


def _dot_nt(a, b):
    return lax.dot_general(a, b, (((1,), (1,)), ((), ())), preferred_element_type=F32)


def _const_spec(shape):
    zeros = (0,) * len(shape)
    return pl.BlockSpec(shape, lambda *_: zeros)


def _inproj_kernel(x_ref, g_ref, w_ref, q_ref, k_ref, v_ref, u_ref, *, d):
    h = _rms(x_ref[...], g_ref[...]).astype(BF16)
    q_ref[...] = (_dot(h, w_ref[:, 0:d]) * (HEAD_DIM ** -0.5)).astype(BF16)
    k_ref[...] = _dot(h, w_ref[:, d:2 * d])
    v_ref[...] = _dot(h, w_ref[:, 2 * d:3 * d])
    u_ref[...] = _dot(h, w_ref[:, 3 * d:])


def _inproj(x, g, w_bf, tile):
    n, d = x.shape
    wu = w_bf.shape[1] - 3 * d
    row = lambda w: pl.BlockSpec((tile, w), lambda i: (i, 0))
    return pl.pallas_call(
        functools.partial(_inproj_kernel, d=d),
        grid=(n // tile,),
        in_specs=[row(d), _const_spec((1, d)), _const_spec(w_bf.shape)],
        out_specs=[row(d), row(d), row(d), row(wu)],
        out_shape=[jax.ShapeDtypeStruct((n, d), BF16), jax.ShapeDtypeStruct((n, d), F32),
                   jax.ShapeDtypeStruct((n, d), F32), jax.ShapeDtypeStruct((n, wu), F32)],
        compiler_params=_cparams("arbitrary"),
        name="inproj",
    )(x, g.reshape(1, d), w_bf)


def _head_norm(o, g, lam_init):
    return _rms(o, g) * (1.0 - lam_init)


def _pattn_kernel(lam_ref, q_ref, k_ref, v_ref, bias_ref, g_ref, o_ref,
                  kb_ref, vb_ref, m_ref, l_ref, acc_ref, *, tq, tk, lam_init):
    qi = pl.program_id(2)

    @pl.when(qi == 0)
    def _():
        kb_ref[...] = k_ref[...].astype(BF16)
        vb_ref[...] = v_ref[...].astype(BF16)

    q = q_ref[...]
    lane = lax.broadcasted_iota(jnp.int32, q.shape, 1)
    zero = jnp.zeros_like(q)
    qq = jnp.concatenate([jnp.where(lane < HEAD_DIM, q, zero), jnp.where(lane >= HEAD_DIM, q, zero)], axis=0)
    m_ref[...] = jnp.full(m_ref.shape, NEG_INF, F32)
    l_ref[...] = jnp.zeros(l_ref.shape, F32)
    acc_ref[...] = jnp.zeros(acc_ref.shape, F32)

    def body(kb, carry):
        start = pl.multiple_of(kb * tk, tk)
        k = kb_ref[pl.ds(start, tk), :]
        v = vb_ref[pl.ds(start, tk), :]
        b = bias_ref[qi - kb]
        s = _dot_nt(qq, k) + jnp.concatenate([b, b], axis=0)
        m_old = m_ref[...]
        m_new = jnp.maximum(m_old, jnp.max(s, axis=1, keepdims=True))
        alpha = jnp.exp(m_old - m_new)
        p = jnp.exp(s - m_new)
        l_ref[...] = alpha * l_ref[...] + jnp.sum(p, axis=1, keepdims=True)
        acc_ref[...] = alpha * acc_ref[...] + _dot(p.astype(BF16), v)
        m_ref[...] = m_new
        return carry

    lax.fori_loop(0, qi + 1, body, 0)
    o = acc_ref[...] / l_ref[...]
    o = o[:tq] - lam_ref[0, 0] * o[tq:]
    o_ref[...] = _head_norm(o, g_ref[...], lam_init).astype(o_ref.dtype)


def _prompt_attention(lam, q, k, v, bias_tiles, subln_g, lam_init, tq):
    b, s, d = q.shape
    nq = s // tq
    blk = lambda rows, im: pl.BlockSpec((None, rows, V_DIM), im)
    return pl.pallas_call(
        functools.partial(_pattn_kernel, tq=tq, tk=tq, lam_init=lam_init),
        grid=(b, N_HEADS, nq),
        in_specs=[pl.BlockSpec(memory_space=pltpu.SMEM),
                  blk(tq, lambda bi, h, qi: (bi, qi, h)),
                  blk(s, lambda bi, h, qi: (bi, 0, h)),
                  blk(s, lambda bi, h, qi: (bi, 0, h)),
                  pl.BlockSpec((None, nq, tq, tq), lambda bi, h, qi: (h, 0, 0, 0)),
                  _const_spec((1, V_DIM))],
        out_specs=blk(tq, lambda bi, h, qi: (bi, qi, h)),
        out_shape=jax.ShapeDtypeStruct((b, s, d), BF16),
        scratch_shapes=[pltpu.VMEM((s, V_DIM), BF16), pltpu.VMEM((s, V_DIM), BF16),
                        pltpu.VMEM((2 * tq, 1), F32), pltpu.VMEM((2 * tq, 1), F32),
                        pltpu.VMEM((2 * tq, V_DIM), F32)],
        compiler_params=_cparams("arbitrary", "arbitrary", "arbitrary"),
        name="prompt_attn",
    )(lam, q, k, v, bias_tiles, subln_g.reshape(1, V_DIM))


def _dattn_kernel(pt_ref, lam_ref, qbd_ref, *refs, n_q, lam_init):
    del pt_ref
    npg = PAGES_PER_STEP
    k_refs, v_refs = refs[:npg], refs[npg:2 * npg]
    bias_ref, kn_ref, vn_ref, biasn_ref, g_ref, o_ref, m_ref, l_ref, acc_ref = refs[2 * npg:]
    g_idx = pl.program_id(1)
    rows = m_ref.shape[0]

    @pl.when(g_idx == 0)
    def _():
        m_ref[...] = jnp.full(m_ref.shape, NEG_INF, F32)
        l_ref[...] = jnp.zeros(l_ref.shape, F32)
        acc_ref[...] = jnp.zeros(acc_ref.shape, F32)

    qbd = qbd_ref[...]

    def scores_t(k):
        return _dot(k, qbd).T[:rows]

    def update(s, vs):
        m_old = m_ref[...]
        m_new = jnp.maximum(m_old, jnp.max(s, axis=1, keepdims=True))
        alpha = jnp.exp(m_old - m_new)
        p = jnp.exp(s - m_new)
        l_ref[...] = alpha * l_ref[...] + jnp.sum(p, axis=1, keepdims=True)
        pb = p.astype(BF16)
        keys = vs[0].shape[0]
        pv = _dot(pb[:, :keys], vs[0])
        for i in range(1, len(vs)):
            pv += _dot(pb[:, i * keys:(i + 1) * keys], vs[i])
        acc_ref[...] = alpha * acc_ref[...] + pv
        m_ref[...] = m_new

    s = jnp.concatenate([scores_t(kr[...].astype(BF16)) for kr in k_refs], axis=1) + bias_ref[...]
    update(s, [vr[...].astype(BF16) for vr in v_refs])

    @pl.when(g_idx == pl.num_programs(1) - 1)
    def _():
        update(scores_t(kn_ref[...]) + biasn_ref[...], [vn_ref[...]])
        o = acc_ref[...] / l_ref[...]
        half = rows // 2
        o = o[:half] - lam_ref[0, 0] * o[half:]
        lane = lax.broadcasted_iota(jnp.int32, (N_HEADS, o.shape[1]), 1)
        lo = lax.broadcasted_iota(jnp.int32, (N_HEADS, o.shape[1]), 0) * V_DIM
        keep = (lane >= lo) & (lane < lo + V_DIM)
        g = g_ref[...]
        out_rows = []
        for j in range(n_q):
            oj = jnp.where(keep, o[j * N_HEADS:(j + 1) * N_HEADS], 0.0)
            ms = jnp.sum(oj * oj, axis=1, keepdims=True) / V_DIM
            oj = oj * lax.rsqrt(ms + RMS_EPS) * g * (1.0 - lam_init)
            out_rows.append(jnp.sum(oj, axis=0, keepdims=True))
        o_ref[...] = jnp.concatenate(out_rows, axis=0)


def _decode_attention(page_table, lam, qbd, cache_k, cache_v, bias_past, k_new, v_new, bias_new,
                      subln_g_tiled, lam_init, n_q):
    nb, n_pages = page_table.shape
    _, page, d = cache_k.shape
    npg = PAGES_PER_STEP
    rows = bias_past.shape[0]

    def page_spec(i):
        return pl.BlockSpec((None, page, d), lambda b, g, pt: (pt[b, g * npg + i], 0, 0))

    seq = lambda r, c: pl.BlockSpec((None, r, c), lambda b, g, pt: (b, 0, 0))
    in_specs = ([pl.BlockSpec(memory_space=pltpu.SMEM), seq(d, LANES)]
                + [page_spec(i) for i in range(npg)] * 2
                + [pl.BlockSpec((rows, npg * page), lambda b, g, pt: (0, g)),
                   seq(LANES, d), seq(LANES, d),
                   pl.BlockSpec((rows, LANES), lambda b, g, pt: (0, 0)),
                   pl.BlockSpec((1, d), lambda b, g, pt: (0, 0))])
    return pl.pallas_call(
        functools.partial(_dattn_kernel, n_q=n_q, lam_init=lam_init),
        grid_spec=pltpu.PrefetchScalarGridSpec(
            num_scalar_prefetch=1,
            grid=(nb, n_pages // npg),
            in_specs=in_specs,
            out_specs=pl.BlockSpec((None, n_q, d), lambda b, g, pt: (b, 0, 0)),
            scratch_shapes=[pltpu.VMEM((rows, 1), F32), pltpu.VMEM((rows, 1), F32),
                            pltpu.VMEM((rows, d), F32)]),
        out_shape=jax.ShapeDtypeStruct((nb, n_q, d), F32),
        compiler_params=_cparams("arbitrary", "arbitrary"),
        name="decode_attn",
    )(page_table, lam, qbd, *([cache_k] * npg), *([cache_v] * npg), bias_past, k_new, v_new, bias_new,
      subln_g_tiled)


def _split_bf16(x):
    hi = x.astype(BF16)
    return hi, (x - hi.astype(F32)).astype(BF16)


def _pool_kernel(u_ref, prev_ref, bc_ref, bp_ref, w_ref, scale_ref, o_ref, *, tile):
    i = pl.program_id(1)
    u = u_ref[...]
    u_hi, u_lo = _split_bf16(u)
    has_prev = (i > 0).astype(F32)
    p_hi, p_lo = _split_bf16(prev_ref[...] * has_prev)
    pos = i * tile + lax.broadcasted_iota(jnp.int32, (tile, 1), 0)
    for g, w in enumerate(POOL_WINDOWS):
        sl = slice(g * POOL_GROUP_DIM, (g + 1) * POOL_GROUP_DIM)
        bc, bp = bc_ref[g], bp_ref[g]
        tot = (_dot(bc, u_hi[:, sl]) + _dot(bc, u_lo[:, sl])) + (_dot(bp, p_hi[:, sl]) + _dot(bp, p_lo[:, sl]))
        cnt = jnp.minimum(pos + 1, w).astype(F32)
        pooled = tot / cnt - u[:, sl]
        mixed = _dot(pooled.astype(BF16), w_ref[g]) * scale_ref[:, sl]
        o_ref[:, sl] = mixed.astype(o_ref.dtype)


def _pool_bands(tile, halo):
    r = jnp.arange(tile, dtype=jnp.int32)[:, None]
    c = jnp.arange(tile, dtype=jnp.int32)[None, :]
    ch = jnp.arange(halo, dtype=jnp.int32)[None, :] - halo
    bc = jnp.stack([((r - c >= 0) & (r - c < w)) for w in POOL_WINDOWS]).astype(BF16)
    bp = jnp.stack([(r - ch < w) for w in POOL_WINDOWS]).astype(BF16)
    return bc, bp


def _pool_mix(u, w_grp_bf, scale, tile):
    b, s, wdt = u.shape
    halo = 16
    bc, bp = _pool_bands(tile, halo)
    per_halo = tile // halo
    return pl.pallas_call(
        functools.partial(_pool_kernel, tile=tile),
        grid=(b, s // tile),
        in_specs=[pl.BlockSpec((None, tile, wdt), lambda bi, i: (bi, i, 0)),
                  pl.BlockSpec((None, halo, wdt), lambda bi, i: (bi, jnp.maximum(i * per_halo - 1, 0), 0)),
                  _const_spec(bc.shape), _const_spec(bp.shape), _const_spec(w_grp_bf.shape),
                  _const_spec((1, wdt))],
        out_specs=pl.BlockSpec((None, tile, wdt), lambda bi, i: (bi, i, 0)),
        out_shape=jax.ShapeDtypeStruct((b, s, wdt), BF16),
        compiler_params=_cparams("arbitrary", "arbitrary"),
        name="pool_mix",
    )(u, u, bc, bp, w_grp_bf, scale.reshape(1, wdt))


def _merge_kernel(x_ref, a_ref, p_ref, g_ref, wg_ref, wa_ref, wp_ref, wo_ref, o_ref, *, d):
    x = x_ref[...]
    h = _rms(x, g_ref[...]).astype(BF16)
    ga = jax.nn.sigmoid(_dot(h, wg_ref[:, :d]))
    merged = ga * _dot(a_ref[...], wa_ref[...])
    gp = jax.nn.sigmoid(_dot(h, wg_ref[:, d:]))
    merged += gp * _dot(p_ref[...], wp_ref[...])
    o_ref[...] = x + _dot(merged.astype(BF16), wo_ref[...])


def _merge(x, a, pooled, g, w_gate, w_br_attn, w_br_pool, w_out, tile):
    n, d = x.shape
    row = lambda w: pl.BlockSpec((tile, w), lambda i: (i, 0))
    return pl.pallas_call(
        functools.partial(_merge_kernel, d=d),
        grid=(n // tile,),
        in_specs=[row(d), row(a.shape[1]), row(pooled.shape[1]), _const_spec((1, d)),
                  _const_spec(w_gate.shape), _const_spec(w_br_attn.shape), _const_spec(w_br_pool.shape),
                  _const_spec(w_out.shape)],
        out_specs=row(d),
        out_shape=jax.ShapeDtypeStruct((n, d), F32),
        compiler_params=_cparams("arbitrary"),
        name="merge",
    )(x, a, pooled, g.reshape(1, d), w_gate, w_br_attn, w_br_pool, w_out)


def _memkv_kernel(x_ref, wk_ref, wv_ref, k_ref, v_ref):
    x = x_ref[...].astype(BF16)
    k_ref[...] = _dot(x, wk_ref[...])
    v_ref[...] = _dot(x, wv_ref[...])


def _memkv(x, wk, wv, tile):
    n, d = x.shape
    row = pl.BlockSpec((tile, d), lambda i: (i, 0))
    return pl.pallas_call(
        _memkv_kernel,
        grid=(n // tile,),
        in_specs=[row, _const_spec(wk.shape), _const_spec(wv.shape)],
        out_specs=[row, row],
        out_shape=[jax.ShapeDtypeStruct((n, d), F32)] * 2,
        compiler_params=_cparams("arbitrary"),
        name="mem_kv",
    )(x, wk, wv)


def _cross_kernel(x_ref, mk_ref, mv_ref, g_ref, wq_ref, wo_ref, o_ref, *, dh):
    x = x_ref[...]
    h = _rms(x, g_ref[...]).astype(BF16)
    q = (_dot(h, wq_ref[...]) * (dh ** -0.5)).astype(BF16)
    out = x
    for hh in range(X_HEADS):
        sl = slice(hh * dh, (hh + 1) * dh)
        s = _dot_nt(q[:, sl], mk_ref[:, sl].astype(BF16))
        p = jnp.exp(s - jnp.max(s, axis=1, keepdims=True))
        o = _dot(p.astype(BF16), mv_ref[:, sl].astype(BF16)) / jnp.sum(p, axis=1, keepdims=True)
        out += _dot(o.astype(BF16), wo_ref[sl, :])
    o_ref[...] = out


def _cross(x, mem_k, mem_v, g, w_cq, w_co, tile):
    b, s, d = x.shape
    m = mem_k.shape[1]
    xs = pl.BlockSpec((None, tile, d), lambda bi, i: (bi, i, 0))
    ms = pl.BlockSpec((None, m, d), lambda bi, i: (bi, 0, 0))
    return pl.pallas_call(
        functools.partial(_cross_kernel, dh=d // X_HEADS),
        grid=(b, s // tile),
        in_specs=[xs, ms, ms, _const_spec((1, d)), _const_spec(w_cq.shape), _const_spec(w_co.shape)],
        out_specs=xs,
        out_shape=jax.ShapeDtypeStruct((b, s, d), F32),
        compiler_params=_cparams("arbitrary", "arbitrary"),
        name="cross_attn",
    )(x, mem_k, mem_v, g.reshape(1, d), w_cq, w_co)


def _router_gates(h, wr_hi, wr_lo, br):
    h_hi, h_lo = _split_bf16(h)
    logits = _dot(h_hi, wr_hi) + (_dot(h_lo, wr_hi) + _dot(h_hi, wr_lo)) + br
    lane = lax.broadcasted_iota(jnp.int32, logits.shape, 1)
    big = jnp.int32(LANES)
    first = lambda mask: jnp.min(jnp.where(mask, lane, big), axis=1, keepdims=True)
    rmax = lambda x: jnp.max(x, axis=1, keepdims=True)

    gmask = (lane >= N_EXPERTS) & (lane < N_EXPERTS + N_GROUPS)
    gl = jnp.where(gmask, logits, NEG_INF)
    gmax = rmax(gl)
    grp = first(gl == gmax) - N_EXPERTS
    p_grp = 1.0 / jnp.sum(jnp.exp(gl - gmax), axis=1, keepdims=True)

    lo = grp * EXPERTS_PER_GROUP
    el = jnp.where((lane >= lo) & (lane < lo + EXPERTS_PER_GROUP), logits, NEG_INF)
    v1 = rmax(el)
    i1 = first(el == v1)
    el2 = jnp.where(lane == i1, NEG_INF, el)
    v2 = rmax(el2)
    i2 = first(el2 == v2)
    t = jnp.exp(v2 - v1)
    w1 = p_grp / (1.0 + t)
    return jnp.where(lane == i1, w1, 0.0) + jnp.where(lane == i2, w1 * t, 0.0)


def _moe_kernel(x_ref, g_ref, wrh_ref, wrl_ref, br_ref, w1_ref, w3_ref, w2_ref, gf_ref, o_ref,
                h_ref, gate_ref, acc_ref):
    e = pl.program_id(1)

    @pl.when(e == 0)
    def _():
        h = _rms(x_ref[...], g_ref[...])
        h_ref[...] = h.astype(BF16)
        gate_ref[...] = _router_gates(h, wrh_ref[...], wrl_ref[...], br_ref[...])
        acc_ref[...] = jnp.zeros(acc_ref.shape, F32)

    h = h_ref[...]
    gates = gate_ref[...]
    lane = lax.broadcasted_iota(jnp.int32, gates.shape, 1)
    gate = jnp.sum(jnp.where(lane == e, gates, 0.0), axis=1, keepdims=True)
    h1 = _dot(h, w1_ref[...])
    h3 = _dot(h, w3_ref[...])
    act = (h1 * jax.nn.sigmoid(h1)) * h3 * gate
    acc_ref[...] += _dot(act.astype(BF16), w2_ref[...])

    @pl.when(e == pl.num_programs(1) - 1)
    def _():
        o_ref[...] = _rms(x_ref[...] + acc_ref[...], gf_ref[...])


def _moe_final(x, g, wr_hi, wr_lo, br, w1, w3, w2, g_final, tile):
    n, d = x.shape
    n_e, _, f = w1.shape
    row = pl.BlockSpec((tile, d), lambda i, e: (i, 0))
    cst = lambda shape: pl.BlockSpec(shape, lambda i, e: (0,) * len(shape))
    return pl.pallas_call(
        _moe_kernel,
        grid=(n // tile, n_e),
        in_specs=[row, cst((1, d)), cst(wr_hi.shape), cst(wr_lo.shape), cst((1, LANES)),
                  pl.BlockSpec((None, d, f), lambda i, e: (e, 0, 0)),
                  pl.BlockSpec((None, d, f), lambda i, e: (e, 0, 0)),
                  pl.BlockSpec((None, f, d), lambda i, e: (e, 0, 0)),
                  cst((1, d))],
        out_specs=row,
        out_shape=jax.ShapeDtypeStruct((n, d), F32),
        scratch_shapes=[pltpu.VMEM((tile, d), BF16), pltpu.VMEM((tile, LANES), F32),
                        pltpu.VMEM((tile, d), F32)],
        compiler_params=_cparams("arbitrary", "arbitrary"),
        name="moe_final",
    )(x, g.reshape(1, d), wr_hi, wr_lo, br, w1, w3, w2, g_final.reshape(1, d))


def _t5_bucket(rel):
    n = jnp.maximum(-rel, 0)
    max_exact = NUM_BUCKETS // 2
    nf = jnp.maximum(n, max_exact).astype(F32)
    large = max_exact + (jnp.log(nf / max_exact) / math.log(MAX_DISTANCE / max_exact)
                         * (NUM_BUCKETS - max_exact)).astype(jnp.int32)
    large = jnp.minimum(large, NUM_BUCKETS - 1)
    return jnp.where(n < max_exact, n, large)


def _masked_bias(rel, rel_bias):
    bias = rel_bias[_t5_bucket(rel)].astype(F32)
    return jnp.where((rel <= 0)[..., None], bias, NEG_INF)


def _prompt_bias_tiles(rel_bias, s, tq):
    nd = s // tq
    delta = jnp.arange(nd, dtype=jnp.int32)[:, None, None]
    qi = jnp.arange(tq, dtype=jnp.int32)[None, :, None]
    kj = jnp.arange(tq, dtype=jnp.int32)[None, None, :]
    rel = kj - (delta * tq + qi)
    return jnp.transpose(_masked_bias(rel, rel_bias), (3, 0, 1, 2))


def _decode_bias(rel_bias, past_len, n_q):
    qpos = past_len + jnp.arange(n_q, dtype=jnp.int32)

    def build(kpos):
        rel = kpos[None, :] - qpos[:, None]
        bias = jnp.transpose(_masked_bias(rel, rel_bias), (0, 2, 1))
        bias = bias.reshape(n_q * N_HEADS, -1)
        return jnp.concatenate([bias, bias], axis=0)

    past = build(jnp.arange(past_len, dtype=jnp.int32))
    new = build(past_len + jnp.arange(n_q, dtype=jnp.int32))
    new = jnp.pad(new, ((0, 0), (0, LANES - n_q)), constant_values=NEG_INF)
    return past, new


def _block_diag_queries(q, n_q):
    nb = q.shape[0] // n_q
    qt = jnp.transpose(q.reshape(nb, n_q, N_HEADS, 2, HEAD_DIM), (0, 2, 3, 4, 1))
    same_h = jnp.eye(N_HEADS, dtype=bool)[:, None, None, None, None, :]
    same_m = jnp.eye(2, dtype=bool)[None, :, None, :, None, None]
    qbd = jnp.where(same_h & same_m, qt[:, :, :, :, None, :, None], jnp.zeros((), q.dtype))
    qbd = qbd.reshape(nb, N_HEADS * 2 * HEAD_DIM, 2 * n_q * N_HEADS)
    return jnp.pad(qbd, ((0, 0), (0, 0), (0, LANES - qbd.shape[-1])))


def _pick_tile(n, pref):
    return pref if n % pref == 0 else n


def kernel(x_prompt, x_sample, mem_prompt, cache_k, cache_v, page_table, cache_mem_k, cache_mem_v, state_pool, rel_bias, norm_mix_g, w_in, lambda_q1, lambda_k1, lambda_q2, lambda_k2, subln_g, w_pool_grp, pool_scale, w_br_attn, w_br_pool, w_gate, w_out, norm_cross_g, w_cq, w_ck, w_cv, w_co, norm_ffn_g, w_router_grp, b_router_grp, w_router_exp, b_router_exp, w_exp_gate, w_exp_up, w_exp_down, norm_final_g):
    depth = w_in.shape[0]
    assert depth == 1, "single-layer step"
    l = 0
    b_p, s_p, d = x_prompt.shape
    b_s, s_s, _ = x_sample.shape
    n_p, n_s = b_p * s_p, b_s * s_s
    n_mem = mem_prompt.shape[1]
    page = cache_k.shape[2]
    past_len = page_table.shape[1] * page
    pool_w = state_pool.shape[-1]

    lam_init = 0.8 - 0.6 * math.exp(-0.3 * l)
    lam = (jnp.exp(jnp.sum(lambda_q1[l].astype(F32) * lambda_k1[l].astype(F32)))
           - jnp.exp(jnp.sum(lambda_q2[l].astype(F32) * lambda_k2[l].astype(F32))) + lam_init).reshape(1, 1)

    bf = lambda w: w.astype(BF16)
    w_in_bf, w_gate_bf, w_ba_bf, w_bp_bf, w_out_bf = bf(w_in[l]), bf(w_gate[l]), bf(w_br_attn[l]), bf(w_br_pool[l]), bf(w_out[l])
    w_cq_bf, w_ck_bf, w_cv_bf, w_co_bf = bf(w_cq[l]), bf(w_ck[l]), bf(w_cv[l]), bf(w_co[l])
    w_grp_bf = bf(w_pool_grp[l])
    w1_bf, w3_bf, w2_bf = bf(w_exp_gate[l]), bf(w_exp_up[l]), bf(w_exp_down[l])
    wr = jnp.concatenate([w_router_exp[l], w_router_grp[l]], axis=1).astype(F32)
    wr = jnp.pad(wr, ((0, 0), (0, LANES - wr.shape[1])))
    wr_hi = wr.astype(BF16)
    wr_lo = (wr - wr_hi.astype(F32)).astype(BF16)
    br = jnp.concatenate([b_router_exp[l], b_router_grp[l]]).astype(F32)
    br = jnp.pad(br, (0, LANES - br.shape[0])).reshape(1, LANES)

    def token_tail(x_tok, a, pooled, mem_k, mem_v, nb, tile, cross_tile):
        x1 = _merge(x_tok, a, pooled, norm_mix_g[l], w_gate_bf, w_ba_bf, w_bp_bf, w_out_bf, tile)
        x1 = x1.reshape(nb, -1, d)
        seq = x1.shape[1]
        x1 = jnp.pad(x1, ((0, 0), (0, -seq % BF16_ROWS), (0, 0)))
        x2 = _cross(x1, mem_k, mem_v, norm_cross_g[l], w_cq_bf, w_co_bf, max(cross_tile, BF16_ROWS))[:, :seq]
        return _moe_final(x2.reshape(-1, d), norm_ffn_g[l], wr_hi, wr_lo, br, w1_bf, w3_bf, w2_bf,
                          norm_final_g, tile)

    tile_p = _pick_tile(n_p, 512)
    tq = _pick_tile(s_p, 256)
    q, k, v, u = _inproj(x_prompt.reshape(n_p, d), norm_mix_g[l], w_in_bf, tile_p)
    bias_tiles = _prompt_bias_tiles(rel_bias, s_p, tq)
    a = _prompt_attention(lam, q.reshape(b_p, s_p, d), k.reshape(b_p, s_p, d), v.reshape(b_p, s_p, d),
                          bias_tiles, subln_g[l], lam_init, tq)
    u3 = u.reshape(b_p, s_p, pool_w)
    pooled = _pool_mix(u3, w_grp_bf, pool_scale[l], _pick_tile(s_p, 256))
    mem_k, mem_v = _memkv(mem_prompt.reshape(b_p * n_mem, d), w_ck_bf, w_cv_bf, _pick_tile(b_p * n_mem, 512))
    mem_k, mem_v = mem_k.reshape(b_p, n_mem, d), mem_v.reshape(b_p, n_mem, d)
    y_prompt = token_tail(x_prompt.reshape(n_p, d), a.reshape(n_p, d), pooled.reshape(n_p, pool_w),
                          mem_k, mem_v, b_p, tile_p, _pick_tile(s_p, 512)).reshape(b_p, s_p, d)
    k_prompt = k.reshape(1, b_p, s_p, N_HEADS, V_DIM)
    v_prompt = v.reshape(1, b_p, s_p, N_HEADS, V_DIM)
    pool_prompt = u3[:, -POOL_CTX:][None]
    x_heads_dim = d // X_HEADS
    mem_k_prompt = mem_k.reshape(1, b_p, n_mem, X_HEADS, x_heads_dim)
    mem_v_prompt = mem_v.reshape(1, b_p, n_mem, X_HEADS, x_heads_dim)

    qs, ks, vs, us = _inproj(x_sample.reshape(n_s, d), norm_mix_g[l], w_in_bf, n_s)
    bias_past, bias_new = _decode_bias(rel_bias, past_len, s_s)
    pad_new = lambda t: jnp.pad(t.reshape(b_s, s_s, d).astype(BF16), ((0, 0), (0, LANES - s_s), (0, 0)))
    a_s = _decode_attention(page_table.astype(jnp.int32), lam, _block_diag_queries(qs, s_s),
                            cache_k[l].reshape(-1, page, d), cache_v[l].reshape(-1, page, d),
                            bias_past, pad_new(ks), pad_new(vs), bias_new,
                            jnp.tile(subln_g[l], N_HEADS).reshape(1, d), lam_init, s_s)
    ctx = state_pool[l].astype(F32)
    full = jnp.concatenate([ctx, us.reshape(b_s, s_s, pool_w)], axis=1)
    seq_pad = -full.shape[1] % 32
    pooled_s = _pool_mix(jnp.pad(full, ((0, 0), (0, seq_pad), (0, 0))), w_grp_bf, pool_scale[l],
                         full.shape[1] + seq_pad)[:, POOL_CTX:POOL_CTX + s_s]
    y_sample = token_tail(x_sample.reshape(n_s, d), a_s.reshape(n_s, d).astype(BF16), pooled_s.reshape(n_s, pool_w),
                          cache_mem_k[l].reshape(b_s, n_mem, d), cache_mem_v[l].reshape(b_s, n_mem, d),
                          b_s, n_s, s_s).reshape(b_s, s_s, d)
    k_sample = ks.reshape(1, b_s, s_s, N_HEADS, V_DIM)
    v_sample = vs.reshape(1, b_s, s_s, N_HEADS, V_DIM)
    pool_sample = full[:, -POOL_CTX:][None]

    return (y_prompt, y_sample, k_prompt, v_prompt, pool_prompt, mem_k_prompt, mem_v_prompt,
            k_sample, v_sample, pool_sample)
```

```python
import functools
import math

import jax
import jax.numpy as jnp
from jax import lax
from jax.experimental import pallas as pl
from jax.experimental.pallas import tpu as pltpu

F32 = jnp.float32
BF16 = jnp.bfloat16

RMS_EPS = 1e-6
N_HEADS = 8
HEAD_DIM = 64
V_DIM = 2 * HEAD_DIM
NUM_BUCKETS = 32
MAX_DISTANCE = 128
POOL_WINDOWS = (2, 4, 8, 16)
POOL_GROUP_DIM = 128
POOL_CTX = max(POOL_WINDOWS) - 1
X_HEADS = 4
N_GROUPS = 4
EXPERTS_PER_GROUP = 8
N_EXPERTS = N_GROUPS * EXPERTS_PER_GROUP
LANES = 128
BF16_ROWS = 16
VMEM_LIMIT = 56 * 1024 * 1024
PAGES_PER_STEP = 4
NEG_INF = float("-inf")


def _cparams(*sem):
    return pltpu.CompilerParams(dimension_semantics=sem, vmem_limit_bytes=VMEM_LIMIT)


def _rms(x, g):
    return x * lax.rsqrt(jnp.mean(x * x, axis=-1, keepdims=True) + RMS_EPS) * g


def _dot(a, b):
    return jnp.dot(a, b, preferred_element_type=F32)


def _dot_nt(a, b):
    return lax.dot_general(a, b, (((1,), (1,)), ((), ())), preferred_element_type=F32)


def _const_spec(shape):
    zeros = (0,) * len(shape)
    return pl.BlockSpec(shape, lambda *_: zeros)


def _inproj_kernel(x_ref, g_ref, w_ref, q_ref, k_ref, v_ref, u_ref, *, d):
    h = _rms(x_ref[...], g_ref[...]).astype(BF16)
    q_ref[...] = (_dot(h, w_ref[:, 0:d]) * (HEAD_DIM ** -0.5)).astype(BF16)
    k_ref[...] = _dot(h, w_ref[:, d:2 * d])
    v_ref[...] = _dot(h, w_ref[:, 2 * d:3 * d])
    u_ref[...] = _dot(h, w_ref[:, 3 * d:])


def _inproj(x, g, w_bf, tile):
    n, d = x.shape
    wu = w_bf.shape[1] - 3 * d
    row = lambda w: pl.BlockSpec((tile, w), lambda i: (i, 0))
    return pl.pallas_call(
        functools.partial(_inproj_kernel, d=d),
        grid=(n // tile,),
        in_specs=[row(d), _const_spec((1, d)), _const_spec(w_bf.shape)],
        out_specs=[row(d), row(d), row(d), row(wu)],
        out_shape=[jax.ShapeDtypeStruct((n, d), BF16), jax.ShapeDtypeStruct((n, d), F32),
                   jax.ShapeDtypeStruct((n, d), F32), jax.ShapeDtypeStruct((n, wu), F32)],
        compiler_params=_cparams("arbitrary"),
        name="inproj",
    )(x, g.reshape(1, d), w_bf)


def _head_norm(o, g, lam_init):
    return _rms(o, g) * (1.0 - lam_init)


def _lane_chunks(x):
    return [x[:, c * LANES:(c + 1) * LANES] for c in range(x.shape[1] // LANES)]


def _pattn_kernel(lam_ref, q_ref, k_ref, v_ref, bias_ref, g_ref, o_ref, kb_ref, vb_ref, *, tq, lam_init):
    qi = pl.program_id(2)

    @pl.when(qi == 0)
    def _():
        kb_ref[...] = k_ref[...].astype(BF16)
        vb_ref[...] = v_ref[...].astype(BF16)

    def block(n):
        q = q_ref[...]
        lane = lax.broadcasted_iota(jnp.int32, q.shape, 1)
        zero = jnp.zeros_like(q)
        tiles = [bias_ref[n - 1 - kb] for kb in range(n)]
        outs = []
        for qm in (jnp.where(lane < HEAD_DIM, q, zero), jnp.where(lane >= HEAD_DIM, q, zero)):
            s = [_dot_nt(qm, kb_ref[kb * tq:(kb + 1) * tq, :]) + tiles[kb] for kb in range(n)]
            chunks = [c for t in s for c in _lane_chunks(t)]
            m = jnp.max(functools.reduce(jnp.maximum, chunks), axis=1, keepdims=True)
            p = [jnp.exp(c - m) for c in chunks]
            l = jnp.sum(functools.reduce(jnp.add, p), axis=1, keepdims=True)
            pv = _dot(jnp.concatenate(p, axis=1).astype(BF16), vb_ref[0:n * tq, :])
            outs.append(pv / l)
        o = outs[0] - lam_ref[0, 0] * outs[1]
        o_ref[...] = _head_norm(o, g_ref[...], lam_init).astype(o_ref.dtype)

    for n in range(1, bias_ref.shape[0] + 1):
        pl.when(qi == n - 1)(functools.partial(block, n))


def _prompt_attention(lam, q, k, v, bias_tiles, subln_g, lam_init, tq):
    b, s, d = q.shape
    nq = s // tq
    blk = lambda rows, im: pl.BlockSpec((None, rows, V_DIM), im)
    return pl.pallas_call(
        functools.partial(_pattn_kernel, tq=tq, lam_init=lam_init),
        grid=(N_HEADS, b, nq),
        in_specs=[pl.BlockSpec(memory_space=pltpu.SMEM),
                  blk(tq, lambda h, bi, qi: (bi, qi, h)),
                  blk(s, lambda h, bi, qi: (bi, 0, h)),
                  blk(s, lambda h, bi, qi: (bi, 0, h)),
                  pl.BlockSpec((None, nq, tq, tq), lambda h, bi, qi: (h, 0, 0, 0)),
                  _const_spec((1, V_DIM))],
        out_specs=blk(tq, lambda h, bi, qi: (bi, qi, h)),
        out_shape=jax.ShapeDtypeStruct((b, s, d), BF16),
        scratch_shapes=[pltpu.VMEM((s, V_DIM), BF16), pltpu.VMEM((s, V_DIM), BF16)],
        compiler_params=_cparams("arbitrary", "arbitrary", "arbitrary"),
        name="prompt_attn",
    )(lam, q, k, v, bias_tiles, subln_g.reshape(1, V_DIM))


def _dattn_kernel(pt_ref, lam_ref, q_ref, *refs, page, lam_init):
    del pt_ref
    npg = PAGES_PER_STEP
    k_refs, v_refs = refs[:npg], refs[npg:2 * npg]
    bias_ref, kn_ref, vn_ref, biasn_ref, g_ref, o_ref, m_ref, l_ref, acc_ref = refs[2 * npg:]
    g_idx, b = pl.program_id(0), pl.program_id(1)
    hr = q_ref.shape[1]
    half = hr // 2

    @pl.when(g_idx == 0)
    def _():
        m_ref[b] = jnp.full(m_ref.shape[1:], NEG_INF, F32)
        l_ref[b] = jnp.zeros(l_ref.shape[1:], F32)
        acc_ref[b] = jnp.zeros(acc_ref.shape[1:], F32)

    def head_rows(ref, h):
        return ref[pl.ds(h, page, stride=N_HEADS), :].astype(BF16)

    def update(s, v_blocks):
        m_old = m_ref[b]
        m_new = jnp.maximum(m_old, jnp.max(s, axis=1, keepdims=True))
        alpha = jnp.exp(m_old - m_new)
        p = jnp.exp(s - m_new)
        l_ref[b] = alpha * l_ref[b] + jnp.sum(p, axis=1, keepdims=True)
        m_ref[b] = m_new
        pb = p.astype(BF16)
        for h in range(N_HEADS):
            rs = slice(h * hr, (h + 1) * hr)
            pv = None
            for i, v in enumerate(v_blocks[h]):
                t = _dot(pb[rs, i * page:(i + 1) * page], v)
                pv = t if pv is None else pv + t
            acc_ref[b, h] = alpha[rs] * acc_ref[b, h] + pv

    s = jnp.concatenate(
        [jnp.concatenate([_dot_nt(q_ref[h], head_rows(kr, h)) for kr in k_refs], axis=1)
         for h in range(N_HEADS)], axis=0)
    update(s + bias_ref[...], [[head_rows(vr, h) for vr in v_refs] for h in range(N_HEADS)])

    @pl.when(g_idx == pl.num_programs(0) - 1)
    def _():
        def pad_page(x):
            return jnp.concatenate([x, jnp.zeros((page - x.shape[0], x.shape[1]), x.dtype)], axis=0).astype(BF16)

        sn = jnp.concatenate([_dot_nt(q_ref[h], pad_page(kn_ref[h])) for h in range(N_HEADS)], axis=0)
        update(sn + biasn_ref[...], [[pad_page(vn_ref[h])] for h in range(N_HEADS)])
        linv = 1.0 / l_ref[b]
        g = g_ref[...]
        for h in range(N_HEADS):
            o = acc_ref[b, h] * linv[h * hr:(h + 1) * hr]
            o = o[:half] - lam_ref[0, 0] * o[half:]
            o_ref[b, :, h * V_DIM:(h + 1) * V_DIM] = _head_norm(o, g, lam_init)


def _decode_attention(page_table, lam, q_heads, cache_k, cache_v, bias_past, k_new, v_new, bias_new,
                      subln_g, lam_init, page):
    nb, n_pages = page_table.shape
    npg = PAGES_PER_STEP
    _, _, hr, _ = q_heads.shape
    slots = hr // 2
    rows = N_HEADS * hr
    d = N_HEADS * V_DIM

    def page_spec(i):
        return pl.BlockSpec((None, page * N_HEADS, V_DIM), lambda g, b, pt: (pt[b, g * npg + i], 0, 0))

    seq = lambda *blk: pl.BlockSpec((None,) + blk, lambda g, b, pt: (b,) + (0,) * len(blk))
    in_specs = ([pl.BlockSpec(memory_space=pltpu.SMEM), seq(N_HEADS, hr, V_DIM)]
                + [page_spec(i) for i in range(npg)] * 2
                + [pl.BlockSpec((rows, npg * page), lambda g, b, pt: (0, g)),
                   seq(N_HEADS, slots, V_DIM), seq(N_HEADS, slots, V_DIM),
                   pl.BlockSpec((rows, page), lambda g, b, pt: (0, 0)),
                   pl.BlockSpec((1, V_DIM), lambda g, b, pt: (0, 0))])
    return pl.pallas_call(
        functools.partial(_dattn_kernel, page=page, lam_init=lam_init),
        grid_spec=pltpu.PrefetchScalarGridSpec(
            num_scalar_prefetch=1,
            grid=(n_pages // npg, nb),
            in_specs=in_specs,
            out_specs=pl.BlockSpec((nb, slots, d), lambda g, b, pt: (0, 0, 0)),
            scratch_shapes=[pltpu.VMEM((nb, rows, 1), F32), pltpu.VMEM((nb, rows, 1), F32),
                            pltpu.VMEM((nb, N_HEADS, hr, V_DIM), F32)]),
        out_shape=jax.ShapeDtypeStruct((nb, slots, d), F32),
        compiler_params=_cparams("arbitrary", "arbitrary"),
        name="decode_attn",
    )(page_table, lam, q_heads, *([cache_k] * npg), *([cache_v] * npg), bias_past, k_new, v_new, bias_new,
      subln_g.reshape(1, V_DIM))


def _split_bf16(x):
    hi = x.astype(BF16)
    return hi, (x - hi.astype(F32)).astype(BF16)


def _pool_kernel(u_ref, prev_ref, bc_ref, bp_ref, w_ref, scale_ref, o_ref, *, tile):
    i = pl.program_id(1)
    u = u_ref[...]
    u_hi, u_lo = _split_bf16(u)
    has_prev = (i > 0).astype(F32)
    p_hi, p_lo = _split_bf16(prev_ref[...] * has_prev)
    pos = i * tile + lax.broadcasted_iota(jnp.int32, (tile, 1), 0)
    for g, w in enumerate(POOL_WINDOWS):
        sl = slice(g * POOL_GROUP_DIM, (g + 1) * POOL_GROUP_DIM)
        bc, bp = bc_ref[g], bp_ref[g]
        tot = (_dot(bc, u_hi[:, sl]) + _dot(bc, u_lo[:, sl])) + (_dot(bp, p_hi[:, sl]) + _dot(bp, p_lo[:, sl]))
        cnt = jnp.minimum(pos + 1, w).astype(F32)
        pooled = tot / cnt - u[:, sl]
        mixed = _dot(pooled.astype(BF16), w_ref[g]) * scale_ref[:, sl]
        o_ref[:, sl] = mixed.astype(o_ref.dtype)


def _pool_bands(tile, halo):
    r = jnp.arange(tile, dtype=jnp.int32)[:, None]
    c = jnp.arange(tile, dtype=jnp.int32)[None, :]
    ch = jnp.arange(halo, dtype=jnp.int32)[None, :] - halo
    bc = jnp.stack([((r - c >= 0) & (r - c < w)) for w in POOL_WINDOWS]).astype(BF16)
    bp = jnp.stack([(r - ch < w) for w in POOL_WINDOWS]).astype(BF16)
    return bc, bp


def _pool_mix(u, w_grp_bf, scale, tile):
    b, s, wdt = u.shape
    halo = 16
    bc, bp = _pool_bands(tile, halo)
    per_halo = tile // halo
    return pl.pallas_call(
        functools.partial(_pool_kernel, tile=tile),
        grid=(b, s // tile),
        in_specs=[pl.BlockSpec((None, tile, wdt), lambda bi, i: (bi, i, 0)),
                  pl.BlockSpec((None, halo, wdt), lambda bi, i: (bi, jnp.maximum(i * per_halo - 1, 0), 0)),
                  _const_spec(bc.shape), _const_spec(bp.shape), _const_spec(w_grp_bf.shape),
                  _const_spec((1, wdt))],
        out_specs=pl.BlockSpec((None, tile, wdt), lambda bi, i: (bi, i, 0)),
        out_shape=jax.ShapeDtypeStruct((b, s, wdt), BF16),
        compiler_params=_cparams("arbitrary", "arbitrary"),
        name="pool_mix",
    )(u, u, bc, bp, w_grp_bf, scale.reshape(1, wdt))


def _merge_kernel(x_ref, a_ref, p_ref, g_ref, wg_ref, wa_ref, wp_ref, wo_ref, o_ref, *, d):
    x = x_ref[...]
    h = _rms(x, g_ref[...]).astype(BF16)
    ga = jax.nn.sigmoid(_dot(h, wg_ref[:, :d]))
    merged = ga * _dot(a_ref[...], wa_ref[...])
    gp = jax.nn.sigmoid(_dot(h, wg_ref[:, d:]))
    merged += gp * _dot(p_ref[...], wp_ref[...])
    o_ref[...] = x + _dot(merged.astype(BF16), wo_ref[...])


def _merge(x, a, pooled, g, w_gate, w_br_attn, w_br_pool, w_out, tile):
    n, d = x.shape
    row = lambda w: pl.BlockSpec((tile, w), lambda i: (i, 0))
    return pl.pallas_call(
        functools.partial(_merge_kernel, d=d),
        grid=(n // tile,),
        in_specs=[row(d), row(a.shape[1]), row(pooled.shape[1]), _const_spec((1, d)),
                  _const_spec(w_gate.shape), _const_spec(w_br_attn.shape), _const_spec(w_br_pool.shape),
                  _const_spec(w_out.shape)],
        out_specs=row(d),
        out_shape=jax.ShapeDtypeStruct((n, d), F32),
        compiler_params=_cparams("arbitrary"),
        name="merge",
    )(x, a, pooled, g.reshape(1, d), w_gate, w_br_attn, w_br_pool, w_out)


def _memkv_kernel(x_ref, wk_ref, wv_ref, k_ref, v_ref):
    x = x_ref[...].astype(BF16)
    k_ref[...] = _dot(x, wk_ref[...])
    v_ref[...] = _dot(x, wv_ref[...])


def _memkv(x, wk, wv, tile):
    n, d = x.shape
    row = pl.BlockSpec((tile, d), lambda i: (i, 0))
    return pl.pallas_call(
        _memkv_kernel,
        grid=(n // tile,),
        in_specs=[row, _const_spec(wk.shape), _const_spec(wv.shape)],
        out_specs=[row, row],
        out_shape=[jax.ShapeDtypeStruct((n, d), F32)] * 2,
        compiler_params=_cparams("arbitrary"),
        name="mem_kv",
    )(x, wk, wv)


def _cross_kernel(x_ref, mk_ref, mv_ref, g_ref, wq_ref, wo_ref, o_ref, *, dh):
    x = x_ref[...]
    h = _rms(x, g_ref[...]).astype(BF16)
    q = (_dot(h, wq_ref[...]) * (dh ** -0.5)).astype(BF16)
    out = x
    for hh in range(X_HEADS):
        sl = slice(hh * dh, (hh + 1) * dh)
        s = _dot_nt(q[:, sl], mk_ref[:, sl].astype(BF16))
        p = jnp.exp(s - jnp.max(s, axis=1, keepdims=True))
        o = _dot(p.astype(BF16), mv_ref[:, sl].astype(BF16)) / jnp.sum(p, axis=1, keepdims=True)
        out += _dot(o.astype(BF16), wo_ref[sl, :])
    o_ref[...] = out


def _cross(x, mem_k, mem_v, g, w_cq, w_co, tile):
    b, s, d = x.shape
    m = mem_k.shape[1]
    xs = pl.BlockSpec((None, tile, d), lambda bi, i: (bi, i, 0))
    ms = pl.BlockSpec((None, m, d), lambda bi, i: (bi, 0, 0))
    return pl.pallas_call(
        functools.partial(_cross_kernel, dh=d // X_HEADS),
        grid=(b, s // tile),
        in_specs=[xs, ms, ms, _const_spec((1, d)), _const_spec(w_cq.shape), _const_spec(w_co.shape)],
        out_specs=xs,
        out_shape=jax.ShapeDtypeStruct((b, s, d), F32),
        compiler_params=_cparams("arbitrary", "arbitrary"),
        name="cross_attn",
    )(x, mem_k, mem_v, g.reshape(1, d), w_cq, w_co)


def _router_gates(h, wr_hi, wr_lo, br):
    h_hi, h_lo = _split_bf16(h)
    logits = _dot(h_hi, wr_hi) + (_dot(h_lo, wr_hi) + _dot(h_hi, wr_lo)) + br
    lane = lax.broadcasted_iota(jnp.int32, logits.shape, 1)
    big = jnp.int32(LANES)
    first = lambda mask: jnp.min(jnp.where(mask, lane, big), axis=1, keepdims=True)
    rmax = lambda x: jnp.max(x, axis=1, keepdims=True)

    gmask = (lane >= N_EXPERTS) & (lane < N_EXPERTS + N_GROUPS)
    gl = jnp.where(gmask, logits, NEG_INF)
    gmax = rmax(gl)
    grp = first(gl == gmax) - N_EXPERTS
    p_grp = 1.0 / jnp.sum(jnp.exp(gl - gmax), axis=1, keepdims=True)

    lo = grp * EXPERTS_PER_GROUP
    el = jnp.where((lane >= lo) & (lane < lo + EXPERTS_PER_GROUP), logits, NEG_INF)
    v1 = rmax(el)
    i1 = first(el == v1)
    el2 = jnp.where(lane == i1, NEG_INF, el)
    v2 = rmax(el2)
    i2 = first(el2 == v2)
    t = jnp.exp(v2 - v1)
    w1 = p_grp / (1.0 + t)
    return jnp.where(lane == i1, w1, 0.0) + jnp.where(lane == i2, w1 * t, 0.0)


def _moe_kernel(x_ref, g_ref, wrh_ref, wrl_ref, br_ref, w1_ref, w3_ref, w2_ref, gf_ref, o_ref,
                h_ref, gate_ref, acc_ref):
    e = pl.program_id(1)

    @pl.when(e == 0)
    def _():
        h = _rms(x_ref[...], g_ref[...])
        h_ref[...] = h.astype(BF16)
        gate_ref[...] = _router_gates(h, wrh_ref[...], wrl_ref[...], br_ref[...])
        acc_ref[...] = jnp.zeros(acc_ref.shape, F32)

    h = h_ref[...]
    gates = gate_ref[...]
    lane = lax.broadcasted_iota(jnp.int32, gates.shape, 1)
    gate = jnp.sum(jnp.where(lane == e, gates, 0.0), axis=1, keepdims=True)
    h1 = _dot(h, w1_ref[...])
    h3 = _dot(h, w3_ref[...])
    act = (h1 * jax.nn.sigmoid(h1)) * h3 * gate
    acc_ref[...] += _dot(act.astype(BF16), w2_ref[...])

    @pl.when(e == pl.num_programs(1) - 1)
    def _():
        o_ref[...] = _rms(x_ref[...] + acc_ref[...], gf_ref[...])


def _moe_final(x, g, wr_hi, wr_lo, br, w1, w3, w2, g_final, tile):
    n, d = x.shape
    n_e, _, f = w1.shape
    row = pl.BlockSpec((tile, d), lambda i, e: (i, 0))
    cst = lambda shape: pl.BlockSpec(shape, lambda i, e: (0,) * len(shape))
    return pl.pallas_call(
        _moe_kernel,
        grid=(n // tile, n_e),
        in_specs=[row, cst((1, d)), cst(wr_hi.shape), cst(wr_lo.shape), cst((1, LANES)),
                  pl.BlockSpec((None, d, f), lambda i, e: (e, 0, 0)),
                  pl.BlockSpec((None, d, f), lambda i, e: (e, 0, 0)),
                  pl.BlockSpec((None, f, d), lambda i, e: (e, 0, 0)),
                  cst((1, d))],
        out_specs=row,
        out_shape=jax.ShapeDtypeStruct((n, d), F32),
        scratch_shapes=[pltpu.VMEM((tile, d), BF16), pltpu.VMEM((tile, LANES), F32),
                        pltpu.VMEM((tile, d), F32)],
        compiler_params=_cparams("arbitrary", "arbitrary"),
        name="moe_final",
    )(x, g.reshape(1, d), wr_hi, wr_lo, br, w1, w3, w2, g_final.reshape(1, d))


def _t5_bucket(rel):
    n = jnp.maximum(-rel, 0)
    max_exact = NUM_BUCKETS // 2
    nf = jnp.maximum(n, max_exact).astype(F32)
    large = max_exact + (jnp.log(nf / max_exact) / math.log(MAX_DISTANCE / max_exact)
                         * (NUM_BUCKETS - max_exact)).astype(jnp.int32)
    large = jnp.minimum(large, NUM_BUCKETS - 1)
    return jnp.where(n < max_exact, n, large)


def _distance_bias(rel_bias, n_dist):
    bucket = _t5_bucket(-jnp.arange(n_dist, dtype=jnp.int32))
    onehot = (bucket[:, None] == jnp.arange(NUM_BUCKETS, dtype=jnp.int32)[None, :]).astype(F32)
    return jnp.dot(onehot, rel_bias.astype(F32), precision=lax.Precision.HIGHEST).T


def _prompt_bias_tiles(rel_bias, s, tq):
    nd = s // tq
    p = 2 * tq
    table = _distance_bias(rel_bias, s)
    ext = jnp.pad(table, ((0, 0), (tq, 1)), constant_values=NEG_INF)
    c = jnp.stack([jnp.concatenate([ext[:, d * tq + 1:d * tq + tq + 1][:, ::-1],
                                    ext[:, d * tq + tq + 1:d * tq + p + 1][:, ::-1]], axis=1)
                   for d in range(nd)], axis=1)
    m = jnp.tile(c, (1, 1, tq))[:, :, :tq * (p - 1)].reshape(N_HEADS, nd, tq, p - 1)
    return m[..., :tq]


def _decode_bias(rel_bias, past_len, slots, n_new, page):
    table = _distance_bias(rel_bias, past_len + slots)
    past = jnp.stack([table[:, j + 1:j + 1 + past_len][:, ::-1] for j in range(slots)], axis=1)
    jq = jnp.arange(slots, dtype=jnp.int32)[:, None]
    jk = jnp.arange(page, dtype=jnp.int32)[None, :]
    visible = (jk <= jq) & (jk < n_new)
    new = jnp.where(visible[None], table[:, jnp.clip(jq - jk, 0, slots - 1)], NEG_INF)
    both = lambda t: jnp.broadcast_to(t[:, None], (N_HEADS, 2) + t.shape[1:]).reshape(N_HEADS * 2 * slots, -1)
    return both(past), both(new)


def _head_queries(q, n_q, slots):
    nb = q.shape[0] // n_q
    qh = jnp.transpose(q.reshape(nb, n_q, N_HEADS, V_DIM), (0, 2, 1, 3))
    qh = jnp.pad(qh, ((0, 0), (0, 0), (0, slots - n_q), (0, 0)))
    first = jnp.arange(V_DIM) < HEAD_DIM
    zero = jnp.zeros((), q.dtype)
    return jnp.concatenate([jnp.where(first, qh, zero), jnp.where(first, zero, qh)], axis=2)


def _head_major(t, n_q, slots):
    nb = t.shape[0] // n_q
    th = jnp.transpose(t.reshape(nb, n_q, N_HEADS, V_DIM), (0, 2, 1, 3))
    return jnp.pad(th, ((0, 0), (0, 0), (0, slots - n_q), (0, 0)))


def _pick_tile(n, pref):
    return pref if n % pref == 0 else n


def kernel(x_prompt, x_sample, mem_prompt, cache_k, cache_v, page_table, cache_mem_k, cache_mem_v, state_pool, rel_bias, norm_mix_g, w_in, lambda_q1, lambda_k1, lambda_q2, lambda_k2, subln_g, w_pool_grp, pool_scale, w_br_attn, w_br_pool, w_gate, w_out, norm_cross_g, w_cq, w_ck, w_cv, w_co, norm_ffn_g, w_router_grp, b_router_grp, w_router_exp, b_router_exp, w_exp_gate, w_exp_up, w_exp_down, norm_final_g):
    depth = w_in.shape[0]
    assert depth == 1, "single-layer step"
    l = 0
    b_p, s_p, d = x_prompt.shape
    b_s, s_s, _ = x_sample.shape
    n_p, n_s = b_p * s_p, b_s * s_s
    n_mem = mem_prompt.shape[1]
    page = cache_k.shape[2]
    past_len = page_table.shape[1] * page
    pool_w = state_pool.shape[-1]

    lam_init = 0.8 - 0.6 * math.exp(-0.3 * l)
    lam = (jnp.exp(jnp.sum(lambda_q1[l].astype(F32) * lambda_k1[l].astype(F32)))
           - jnp.exp(jnp.sum(lambda_q2[l].astype(F32) * lambda_k2[l].astype(F32))) + lam_init).reshape(1, 1)

    bf = lambda w: w.astype(BF16)
    w_in_bf, w_gate_bf, w_ba_bf, w_bp_bf, w_out_bf = bf(w_in[l]), bf(w_gate[l]), bf(w_br_attn[l]), bf(w_br_pool[l]), bf(w_out[l])
    w_cq_bf, w_ck_bf, w_cv_bf, w_co_bf = bf(w_cq[l]), bf(w_ck[l]), bf(w_cv[l]), bf(w_co[l])
    w_grp_bf = bf(w_pool_grp[l])
    w1_bf, w3_bf, w2_bf = bf(w_exp_gate[l]), bf(w_exp_up[l]), bf(w_exp_down[l])
    wr = jnp.concatenate([w_router_exp[l], w_router_grp[l]], axis=1).astype(F32)
    wr = jnp.pad(wr, ((0, 0), (0, LANES - wr.shape[1])))
    wr_hi = wr.astype(BF16)
    wr_lo = (wr - wr_hi.astype(F32)).astype(BF16)
    br = jnp.concatenate([b_router_exp[l], b_router_grp[l]]).astype(F32)
    br = jnp.pad(br, (0, LANES - br.shape[0])).reshape(1, LANES)

    def token_tail(x_tok, a, pooled, mem_k, mem_v, nb, tile, cross_tile):
        x1 = _merge(x_tok, a, pooled, norm_mix_g[l], w_gate_bf, w_ba_bf, w_bp_bf, w_out_bf, tile)
        x1 = x1.reshape(nb, -1, d)
        seq = x1.shape[1]
        x1 = jnp.pad(x1, ((0, 0), (0, -seq % BF16_ROWS), (0, 0)))
        x2 = _cross(x1, mem_k, mem_v, norm_cross_g[l], w_cq_bf, w_co_bf, max(cross_tile, BF16_ROWS))[:, :seq]
        return _moe_final(x2.reshape(-1, d), norm_ffn_g[l], wr_hi, wr_lo, br, w1_bf, w3_bf, w2_bf,
                          norm_final_g, tile)

    tile_p = _pick_tile(n_p, 512)
    tq = _pick_tile(s_p, 256)
    q, k, v, u = _inproj(x_prompt.reshape(n_p, d), norm_mix_g[l], w_in_bf, tile_p)
    bias_tiles = _prompt_bias_tiles(rel_bias, s_p, tq)
    a = _prompt_attention(lam, q.reshape(b_p, s_p, d), k.reshape(b_p, s_p, d), v.reshape(b_p, s_p, d),
                          bias_tiles, subln_g[l], lam_init, tq)
    u3 = u.reshape(b_p, s_p, pool_w)
    pooled = _pool_mix(u3, w_grp_bf, pool_scale[l], _pick_tile(s_p, 256))
    mem_k, mem_v = _memkv(mem_prompt.reshape(b_p * n_mem, d), w_ck_bf, w_cv_bf, _pick_tile(b_p * n_mem, 512))
    mem_k, mem_v = mem_k.reshape(b_p, n_mem, d), mem_v.reshape(b_p, n_mem, d)
    y_prompt = token_tail(x_prompt.reshape(n_p, d), a.reshape(n_p, d), pooled.reshape(n_p, pool_w),
                          mem_k, mem_v, b_p, tile_p, _pick_tile(s_p, 512)).reshape(b_p, s_p, d)
    k_prompt = k.reshape(1, b_p, s_p, N_HEADS, V_DIM)
    v_prompt = v.reshape(1, b_p, s_p, N_HEADS, V_DIM)
    pool_prompt = u3[:, -POOL_CTX:][None]
    x_heads_dim = d // X_HEADS
    mem_k_prompt = mem_k.reshape(1, b_p, n_mem, X_HEADS, x_heads_dim)
    mem_v_prompt = mem_v.reshape(1, b_p, n_mem, X_HEADS, x_heads_dim)

    qs, ks, vs, us = _inproj(x_sample.reshape(n_s, d), norm_mix_g[l], w_in_bf, n_s)
    slots = -(-s_s // 8) * 8
    bias_past, bias_new = _decode_bias(rel_bias, past_len, slots, s_s, page)
    paged = lambda c: c.reshape(-1, page * N_HEADS, V_DIM)
    a_s = _decode_attention(page_table.astype(jnp.int32), lam, _head_queries(qs, s_s, slots),
                            paged(cache_k), paged(cache_v), bias_past,
                            _head_major(ks, s_s, slots), _head_major(vs, s_s, slots), bias_new,
                            subln_g[l], lam_init, page)[:, :s_s]
    ctx = state_pool[l].astype(F32)
    full = jnp.concatenate([ctx, us.reshape(b_s, s_s, pool_w)], axis=1)
    seq_pad = -full.shape[1] % 32
    pooled_s = _pool_mix(jnp.pad(full, ((0, 0), (0, seq_pad), (0, 0))), w_grp_bf, pool_scale[l],
                         full.shape[1] + seq_pad)[:, POOL_CTX:POOL_CTX + s_s]
    y_sample = token_tail(x_sample.reshape(n_s, d), a_s.reshape(n_s, d).astype(BF16), pooled_s.reshape(n_s, pool_w),
                          cache_mem_k[l].reshape(b_s, n_mem, d), cache_mem_v[l].reshape(b_s, n_mem, d),
                          b_s, n_s, s_s).reshape(b_s, s_s, d)
    k_sample = ks.reshape(1, b_s, s_s, N_HEADS, V_DIM)
    v_sample = vs.reshape(1, b_s, s_s, N_HEADS, V_DIM)
    pool_sample = full[:, -POOL_CTX:][None]

    return (y_prompt, y_sample, k_prompt, v_prompt, pool_prompt, mem_k_prompt, mem_v_prompt,
            k_sample, v_sample, pool_sample)
```

```python
import functools
import math

import jax
import jax.numpy as jnp
from jax import lax
from jax.experimental import pallas as pl
from jax.experimental.pallas import tpu as pltpu

F32 = jnp.float32
BF16 = jnp.bfloat16

RMS_EPS = 1e-6
N_HEADS = 8
HEAD_DIM = 64
V_DIM = 2 * HEAD_DIM
NUM_BUCKETS = 32
MAX_DISTANCE = 128
POOL_WINDOWS = (2, 4, 8, 16)
POOL_GROUP_DIM = 128
POOL_CTX = max(POOL_WINDOWS) - 1
X_HEADS = 4
N_GROUPS = 4
EXPERTS_PER_GROUP = 8
N_EXPERTS = N_GROUPS * EXPERTS_PER_GROUP
LANES = 128
BF16_ROWS = 16
VMEM_LIMIT = 56 * 1024 * 1024
PAGES_PER_STEP = 4
MOE_BLOCK = 288
NEG_INF = float("-inf")


def _cparams(*sem):
    return pltpu.CompilerParams(dimension_semantics=sem, vmem_limit_bytes=VMEM_LIMIT)


def _rms(x, g):
    return x * lax.rsqrt(jnp.mean(x * x, axis=-1, keepdims=True) + RMS_EPS) * g


def _dot(a, b):
    return jnp.dot(a, b, preferred_element_type=F32)


def _dot_nt(a, b):
    return lax.dot_general(a, b, (((1,), (1,)), ((), ())), preferred_element_type=F32)


def _const_spec(shape):
    zeros = (0,) * len(shape)
    return pl.BlockSpec(shape, lambda *_: zeros)


def _inproj_kernel(x_ref, g_ref, w_ref, q_ref, k_ref, v_ref, u_ref, *, d):
    h = _rms(x_ref[...], g_ref[...]).astype(BF16)
    q_ref[...] = (_dot(h, w_ref[:, 0:d]) * (HEAD_DIM ** -0.5)).astype(BF16)
    k_ref[...] = _dot(h, w_ref[:, d:2 * d])
    v_ref[...] = _dot(h, w_ref[:, 2 * d:3 * d])
    u_ref[...] = _dot(h, w_ref[:, 3 * d:])


def _inproj(x, g, w_bf, tile):
    n, d = x.shape
    wu = w_bf.shape[1] - 3 * d
    row = lambda w: pl.BlockSpec((tile, w), lambda i: (i, 0))
    return pl.pallas_call(
        functools.partial(_inproj_kernel, d=d),
        grid=(n // tile,),
        in_specs=[row(d), _const_spec((1, d)), _const_spec(w_bf.shape)],
        out_specs=[row(d), row(d), row(d), row(wu)],
        out_shape=[jax.ShapeDtypeStruct((n, d), BF16), jax.ShapeDtypeStruct((n, d), F32),
                   jax.ShapeDtypeStruct((n, d), F32), jax.ShapeDtypeStruct((n, wu), F32)],
        compiler_params=_cparams("arbitrary"),
        name="inproj",
    )(x, g.reshape(1, d), w_bf)


def _head_norm(o, g, lam_init):
    return _rms(o, g) * (1.0 - lam_init)


def _lane_chunks(x):
    return [x[:, c * LANES:(c + 1) * LANES] for c in range(x.shape[1] // LANES)]


def _pattn_kernel(lam_ref, q_ref, k_ref, v_ref, bias_ref, g_ref, o_ref, kb_ref, vb_ref, *, tq, lam_init):
    qi = pl.program_id(2)

    @pl.when(qi == 0)
    def _():
        kb_ref[...] = k_ref[...].astype(BF16)
        vb_ref[...] = v_ref[...].astype(BF16)

    def block(n):
        q = q_ref[...]
        lane = lax.broadcasted_iota(jnp.int32, q.shape, 1)
        zero = jnp.zeros_like(q)
        tiles = [bias_ref[n - 1 - kb] for kb in range(n)]
        outs = []
        for qm in (jnp.where(lane < HEAD_DIM, q, zero), jnp.where(lane >= HEAD_DIM, q, zero)):
            s = [_dot_nt(qm, kb_ref[kb * tq:(kb + 1) * tq, :]) + tiles[kb] for kb in range(n)]
            chunks = [c for t in s for c in _lane_chunks(t)]
            m = jnp.max(functools.reduce(jnp.maximum, chunks), axis=1, keepdims=True)
            p = [jnp.exp(c - m) for c in chunks]
            l = jnp.sum(functools.reduce(jnp.add, p), axis=1, keepdims=True)
            pv = _dot(jnp.concatenate(p, axis=1).astype(BF16), vb_ref[0:n * tq, :])
            outs.append(pv / l)
        o = outs[0] - lam_ref[0, 0] * outs[1]
        o_ref[...] = _head_norm(o, g_ref[...], lam_init).astype(o_ref.dtype)

    for n in range(1, bias_ref.shape[0] + 1):
        pl.when(qi == n - 1)(functools.partial(block, n))


def _prompt_attention(lam, q, k, v, bias_tiles, subln_g, lam_init, tq):
    b, s, d = q.shape
    nq = s // tq
    blk = lambda rows, im: pl.BlockSpec((None, rows, V_DIM), im)
    return pl.pallas_call(
        functools.partial(_pattn_kernel, tq=tq, lam_init=lam_init),
        grid=(N_HEADS, b, nq),
        in_specs=[pl.BlockSpec(memory_space=pltpu.SMEM),
                  blk(tq, lambda h, bi, qi: (bi, qi, h)),
                  blk(s, lambda h, bi, qi: (bi, 0, h)),
                  blk(s, lambda h, bi, qi: (bi, 0, h)),
                  pl.BlockSpec((None, nq, tq, tq), lambda h, bi, qi: (h, 0, 0, 0)),
                  _const_spec((1, V_DIM))],
        out_specs=blk(tq, lambda h, bi, qi: (bi, qi, h)),
        out_shape=jax.ShapeDtypeStruct((b, s, d), BF16),
        scratch_shapes=[pltpu.VMEM((s, V_DIM), BF16), pltpu.VMEM((s, V_DIM), BF16)],
        compiler_params=_cparams("arbitrary", "arbitrary", "arbitrary"),
        name="prompt_attn",
    )(lam, q, k, v, bias_tiles, subln_g.reshape(1, V_DIM))


def _dattn_kernel(pt_ref, lam_ref, q_ref, *refs, page, lam_init):
    del pt_ref
    npg = PAGES_PER_STEP
    k_refs, v_refs = refs[:npg], refs[npg:2 * npg]
    bias_ref, kn_ref, vn_ref, biasn_ref, g_ref, o_ref, m_ref, l_ref, acc_ref = refs[2 * npg:]
    g_idx, b = pl.program_id(0), pl.program_id(1)
    hr = q_ref.shape[1]
    half = hr // 2

    @pl.when(g_idx == 0)
    def _():
        m_ref[b] = jnp.full(m_ref.shape[1:], NEG_INF, F32)
        l_ref[b] = jnp.zeros(l_ref.shape[1:], F32)
        acc_ref[b] = jnp.zeros(acc_ref.shape[1:], F32)

    def head_rows(ref, h):
        return ref[pl.ds(h, page, stride=N_HEADS), :].astype(BF16)

    def update(s, v_blocks):
        m_old = m_ref[b]
        m_new = jnp.maximum(m_old, jnp.max(s, axis=1, keepdims=True))
        alpha = jnp.exp(m_old - m_new)
        p = jnp.exp(s - m_new)
        l_ref[b] = alpha * l_ref[b] + jnp.sum(p, axis=1, keepdims=True)
        m_ref[b] = m_new
        pb = p.astype(BF16)
        for h in range(N_HEADS):
            rs = slice(h * hr, (h + 1) * hr)
            pv = None
            for i, v in enumerate(v_blocks[h]):
                t = _dot(pb[rs, i * page:(i + 1) * page], v)
                pv = t if pv is None else pv + t
            acc_ref[b, h] = alpha[rs] * acc_ref[b, h] + pv

    s = jnp.concatenate(
        [jnp.concatenate([_dot_nt(q_ref[h], head_rows(kr, h)) for kr in k_refs], axis=1)
         for h in range(N_HEADS)], axis=0)
    update(s + bias_ref[...], [[head_rows(vr, h) for vr in v_refs] for h in range(N_HEADS)])

    @pl.when(g_idx == pl.num_programs(0) - 1)
    def _():
        def pad_page(x):
            return jnp.concatenate([x, jnp.zeros((page - x.shape[0], x.shape[1]), x.dtype)], axis=0).astype(BF16)

        sn = jnp.concatenate([_dot_nt(q_ref[h], pad_page(kn_ref[h])) for h in range(N_HEADS)], axis=0)
        update(sn + biasn_ref[...], [[pad_page(vn_ref[h])] for h in range(N_HEADS)])
        linv = 1.0 / l_ref[b]
        g = g_ref[...]
        for h in range(N_HEADS):
            o = acc_ref[b, h] * linv[h * hr:(h + 1) * hr]
            o = o[:half] - lam_ref[0, 0] * o[half:]
            o_ref[b, :, h * V_DIM:(h + 1) * V_DIM] = _head_norm(o, g, lam_init)


def _decode_attention(page_table, lam, q_heads, cache_k, cache_v, bias_past, k_new, v_new, bias_new,
                      subln_g, lam_init, page):
    nb, n_pages = page_table.shape
    npg = PAGES_PER_STEP
    _, _, hr, _ = q_heads.shape
    slots = hr // 2
    rows = N_HEADS * hr
    d = N_HEADS * V_DIM

    def page_spec(i):
        return pl.BlockSpec((None, page * N_HEADS, V_DIM), lambda g, b, pt: (pt[b, g * npg + i], 0, 0))

    seq = lambda *blk: pl.BlockSpec((None,) + blk, lambda g, b, pt: (b,) + (0,) * len(blk))
    in_specs = ([pl.BlockSpec(memory_space=pltpu.SMEM), seq(N_HEADS, hr, V_DIM)]
                + [page_spec(i) for i in range(npg)] * 2
                + [pl.BlockSpec((rows, npg * page), lambda g, b, pt: (0, g)),
                   seq(N_HEADS, slots, V_DIM), seq(N_HEADS, slots, V_DIM),
                   pl.BlockSpec((rows, page), lambda g, b, pt: (0, 0)),
                   pl.BlockSpec((1, V_DIM), lambda g, b, pt: (0, 0))])
    return pl.pallas_call(
        functools.partial(_dattn_kernel, page=page, lam_init=lam_init),
        grid_spec=pltpu.PrefetchScalarGridSpec(
            num_scalar_prefetch=1,
            grid=(n_pages // npg, nb),
            in_specs=in_specs,
            out_specs=pl.BlockSpec((nb, slots, d), lambda g, b, pt: (0, 0, 0)),
            scratch_shapes=[pltpu.VMEM((nb, rows, 1), F32), pltpu.VMEM((nb, rows, 1), F32),
                            pltpu.VMEM((nb, N_HEADS, hr, V_DIM), F32)]),
        out_shape=jax.ShapeDtypeStruct((nb, slots, d), F32),
        compiler_params=_cparams("arbitrary", "arbitrary"),
        name="decode_attn",
    )(page_table, lam, q_heads, *([cache_k] * npg), *([cache_v] * npg), bias_past, k_new, v_new, bias_new,
      subln_g.reshape(1, V_DIM))


def _split_bf16(x):
    hi = x.astype(BF16)
    return hi, (x - hi.astype(F32)).astype(BF16)


def _pool_kernel(u_ref, prev_ref, bc_ref, bp_ref, w_ref, scale_ref, o_ref, *, tile):
    i = pl.program_id(1)
    u = u_ref[...]
    u_hi, u_lo = _split_bf16(u)
    has_prev = (i > 0).astype(F32)
    p_hi, p_lo = _split_bf16(prev_ref[...] * has_prev)
    pos = i * tile + lax.broadcasted_iota(jnp.int32, (tile, 1), 0)
    for g, w in enumerate(POOL_WINDOWS):
        sl = slice(g * POOL_GROUP_DIM, (g + 1) * POOL_GROUP_DIM)
        bc, bp = bc_ref[g], bp_ref[g]
        tot = (_dot(bc, u_hi[:, sl]) + _dot(bc, u_lo[:, sl])) + (_dot(bp, p_hi[:, sl]) + _dot(bp, p_lo[:, sl]))
        cnt = jnp.minimum(pos + 1, w).astype(F32)
        pooled = tot / cnt - u[:, sl]
        mixed = _dot(pooled.astype(BF16), w_ref[g]) * scale_ref[:, sl]
        o_ref[:, sl] = mixed.astype(o_ref.dtype)


def _pool_bands(tile, halo):
    r = jnp.arange(tile, dtype=jnp.int32)[:, None]
    c = jnp.arange(tile, dtype=jnp.int32)[None, :]
    ch = jnp.arange(halo, dtype=jnp.int32)[None, :] - halo
    bc = jnp.stack([((r - c >= 0) & (r - c < w)) for w in POOL_WINDOWS]).astype(BF16)
    bp = jnp.stack([(r - ch < w) for w in POOL_WINDOWS]).astype(BF16)
    return bc, bp


def _pool_mix(u, w_grp_bf, scale, tile):
    b, s, wdt = u.shape
    halo = 16
    bc, bp = _pool_bands(tile, halo)
    per_halo = tile // halo
    return pl.pallas_call(
        functools.partial(_pool_kernel, tile=tile),
        grid=(b, s // tile),
        in_specs=[pl.BlockSpec((None, tile, wdt), lambda bi, i: (bi, i, 0)),
                  pl.BlockSpec((None, halo, wdt), lambda bi, i: (bi, jnp.maximum(i * per_halo - 1, 0), 0)),
                  _const_spec(bc.shape), _const_spec(bp.shape), _const_spec(w_grp_bf.shape),
                  _const_spec((1, wdt))],
        out_specs=pl.BlockSpec((None, tile, wdt), lambda bi, i: (bi, i, 0)),
        out_shape=jax.ShapeDtypeStruct((b, s, wdt), BF16),
        compiler_params=_cparams("arbitrary", "arbitrary"),
        name="pool_mix",
    )(u, u, bc, bp, w_grp_bf, scale.reshape(1, wdt))


def _merge_kernel(x_ref, a_ref, p_ref, g_ref, wg_ref, wa_ref, wp_ref, wo_ref, o_ref, *, d):
    x = x_ref[...]
    h = _rms(x, g_ref[...]).astype(BF16)
    ga = jax.nn.sigmoid(_dot(h, wg_ref[:, :d]))
    merged = ga * _dot(a_ref[...], wa_ref[...])
    gp = jax.nn.sigmoid(_dot(h, wg_ref[:, d:]))
    merged += gp * _dot(p_ref[...], wp_ref[...])
    o_ref[...] = x + _dot(merged.astype(BF16), wo_ref[...])


def _merge(x, a, pooled, g, w_gate, w_br_attn, w_br_pool, w_out, tile):
    n, d = x.shape
    row = lambda w: pl.BlockSpec((tile, w), lambda i: (i, 0))
    return pl.pallas_call(
        functools.partial(_merge_kernel, d=d),
        grid=(n // tile,),
        in_specs=[row(d), row(a.shape[1]), row(pooled.shape[1]), _const_spec((1, d)),
                  _const_spec(w_gate.shape), _const_spec(w_br_attn.shape), _const_spec(w_br_pool.shape),
                  _const_spec(w_out.shape)],
        out_specs=row(d),
        out_shape=jax.ShapeDtypeStruct((n, d), F32),
        compiler_params=_cparams("arbitrary"),
        name="merge",
    )(x, a, pooled, g.reshape(1, d), w_gate, w_br_attn, w_br_pool, w_out)


def _memkv_kernel(x_ref, wk_ref, wv_ref, k_ref, v_ref):
    x = x_ref[...].astype(BF16)
    k_ref[...] = _dot(x, wk_ref[...])
    v_ref[...] = _dot(x, wv_ref[...])


def _memkv(x, wk, wv, tile):
    n, d = x.shape
    row = pl.BlockSpec((tile, d), lambda i: (i, 0))
    return pl.pallas_call(
        _memkv_kernel,
        grid=(n // tile,),
        in_specs=[row, _const_spec(wk.shape), _const_spec(wv.shape)],
        out_specs=[row, row],
        out_shape=[jax.ShapeDtypeStruct((n, d), F32)] * 2,
        compiler_params=_cparams("arbitrary"),
        name="mem_kv",
    )(x, wk, wv)


def _cross_kernel(x_ref, mk_ref, mv_ref, g_ref, wq_ref, wo_ref, o_ref, *, dh):
    x = x_ref[...]
    h = _rms(x, g_ref[...]).astype(BF16)
    q = (_dot(h, wq_ref[...]) * (dh ** -0.5)).astype(BF16)
    out = x
    for hh in range(X_HEADS):
        sl = slice(hh * dh, (hh + 1) * dh)
        s = _dot_nt(q[:, sl], mk_ref[:, sl].astype(BF16))
        p = jnp.exp(s - jnp.max(s, axis=1, keepdims=True))
        o = _dot(p.astype(BF16), mv_ref[:, sl].astype(BF16)) / jnp.sum(p, axis=1, keepdims=True)
        out += _dot(o.astype(BF16), wo_ref[sl, :])
    o_ref[...] = out


def _cross(x, mem_k, mem_v, g, w_cq, w_co, tile):
    b, s, d = x.shape
    m = mem_k.shape[1]
    xs = pl.BlockSpec((None, tile, d), lambda bi, i: (bi, i, 0))
    ms = pl.BlockSpec((None, m, d), lambda bi, i: (bi, 0, 0))
    return pl.pallas_call(
        functools.partial(_cross_kernel, dh=d // X_HEADS),
        grid=(b, s // tile),
        in_specs=[xs, ms, ms, _const_spec((1, d)), _const_spec(w_cq.shape), _const_spec(w_co.shape)],
        out_specs=xs,
        out_shape=jax.ShapeDtypeStruct((b, s, d), F32),
        compiler_params=_cparams("arbitrary", "arbitrary"),
        name="cross_attn",
    )(x, mem_k, mem_v, g.reshape(1, d), w_cq, w_co)


def _router_gates(h, wr_hi, wr_lo, br):
    h_hi, h_lo = _split_bf16(h)
    logits = _dot(h_hi, wr_hi) + (_dot(h_lo, wr_hi) + _dot(h_hi, wr_lo)) + br
    lane = lax.broadcasted_iota(jnp.int32, logits.shape, 1)
    big = jnp.int32(LANES)
    first = lambda mask: jnp.min(jnp.where(mask, lane, big), axis=1, keepdims=True)
    rmax = lambda x: jnp.max(x, axis=1, keepdims=True)

    gmask = (lane >= N_EXPERTS) & (lane < N_EXPERTS + N_GROUPS)
    gl = jnp.where(gmask, logits, NEG_INF)
    gmax = rmax(gl)
    grp = first(gl == gmax) - N_EXPERTS
    p_grp = 1.0 / jnp.sum(jnp.exp(gl - gmax), axis=1, keepdims=True)

    lo = grp * EXPERTS_PER_GROUP
    el = jnp.where((lane >= lo) & (lane < lo + EXPERTS_PER_GROUP), logits, NEG_INF)
    v1 = rmax(el)
    i1 = first(el == v1)
    el2 = jnp.where(lane == i1, NEG_INF, el)
    v2 = rmax(el2)
    i2 = first(el2 == v2)
    t = jnp.exp(v2 - v1)
    w1 = p_grp / (1.0 + t)
    return jnp.where(lane == i1, w1, 0.0) + jnp.where(lane == i2, w1 * t, 0.0), grp


def _split3_bf16(x):
    a = x.astype(BF16)
    r = x - a.astype(F32)
    b = r.astype(BF16)
    return a, b, (r - b.astype(F32)).astype(BF16)


def _moe_kernel(x_ref, g_ref, wrh_ref, wrl_ref, br_ref, w1_ref, w3_ref, w2_ref, gf_ref, o_ref,
                hs_ref, gs_ref, ys_ref, pt_ref, meta_ref, *, tile, tp, n_sub):
    sg = pl.program_id(1)
    per_step = EXPERTS_PER_GROUP // n_sub

    @pl.when(sg == 0)
    def _route():
        h = _rms(x_ref[...], g_ref[...])
        gates, grp = _router_gates(h, wrh_ref[...], wrl_ref[...], br_ref[...])
        lane = lax.broadcasted_iota(jnp.int32, gates.shape, 1)
        member = lane == grp
        r = lax.broadcasted_iota(jnp.int32, (tile, tile), 0)
        c = lax.broadcasted_iota(jnp.int32, (tile, tile), 1)
        tri = jnp.where(r >= c, 1.0, 0.0).astype(BF16)
        csum = _dot(tri, jnp.where(member, 1.0, 0.0).astype(BF16))
        cnt = csum[tile - 1:tile, :]
        padded = jnp.ceil(cnt * (1.0 / BF16_ROWS)) * BF16_ROWS
        lane_row = lane[0:1, :]
        off = jnp.zeros_like(cnt)
        run = jnp.zeros((1, 1), F32)
        for gg in range(N_GROUPS):
            off = jnp.where(lane_row == gg, run, off)
            run = run + padded[:, gg:gg + 1]
        pos = jnp.sum(jnp.where(member, off + csum - 1.0, 0.0), axis=1, keepdims=True)
        pos_row = jnp.transpose(jnp.broadcast_to(pos, (tile, LANES)))[0:1, :]
        pt = jnp.where(lax.broadcasted_iota(jnp.int32, (tile, tp), 1) == pos.astype(jnp.int32), 1.0, 0.0)
        p = jnp.where(lax.broadcasted_iota(jnp.int32, (tp, tile), 0) == pos_row.astype(jnp.int32), 1.0, 0.0)
        p = p.astype(BF16)
        pt_ref[...] = pt.astype(BF16)
        hs_ref[0:tp, :] = _dot(p, h.astype(BF16)).astype(BF16)
        hs_ref[tp:, :] = jnp.zeros((hs_ref.shape[0] - tp, hs_ref.shape[1]), BF16)
        grel = gates
        for gg in range(1, N_GROUPS):
            grel = jnp.where(grp == gg, pltpu.roll(gates, LANES - gg * EXPERTS_PER_GROUP, axis=1), grel)
        gs_ref[0:tp, :] = functools.reduce(jnp.add, [_dot(p, t) for t in _split3_bf16(grel)])
        gs_ref[tp:, :] = jnp.zeros((gs_ref.shape[0] - tp, gs_ref.shape[1]), F32)
        ys_ref[...] = jnp.zeros(ys_ref.shape, F32)
        for gg in range(N_GROUPS):
            pick = lambda v: jnp.sum(jnp.where(lane_row == gg, v, 0.0)).astype(jnp.int32)
            meta_ref[0, gg] = pick(off)
            meta_ref[1, gg] = pick(cnt)

    grp_id = sg // n_sub
    e0 = (sg % n_sub) * per_step
    start = meta_ref[0, grp_id]
    cnt = meta_ref[1, grp_id]
    lane8 = lax.broadcasted_iota(jnp.int32, (MOE_BLOCK, LANES), 1)

    def block(i, carry):
        r0 = pl.multiple_of(start + i * MOE_BLOCK, BF16_ROWS)
        rows = hs_ref[pl.ds(r0, MOE_BLOCK), :]
        ridx = r0 + lax.broadcasted_iota(jnp.int32, (MOE_BLOCK, 1), 0)
        gts = jnp.where(ridx < start + cnt, gs_ref[pl.ds(r0, MOE_BLOCK), :], 0.0)
        y = None
        for j in range(per_step):
            gate = jnp.sum(jnp.where(lane8 == e0 + j, gts, 0.0), axis=1, keepdims=True)
            h1 = _dot(rows, w1_ref[j])
            h3 = _dot(rows, w3_ref[j])
            act = (h1 * jax.nn.sigmoid(h1)) * h3 * gate
            t = _dot(act.astype(BF16), w2_ref[j])
            y = t if y is None else y + t
        ys_ref[pl.ds(r0, MOE_BLOCK), :] += y
        return carry

    lax.fori_loop(0, (cnt + MOE_BLOCK - 1) // MOE_BLOCK, block, 0)

    @pl.when(sg == pl.num_programs(1) - 1)
    def _finish():
        pt = pt_ref[...]
        y_hi, y_lo = _split_bf16(ys_ref[0:tp, :])
        o_ref[...] = _rms(x_ref[...] + (_dot(pt, y_hi) + _dot(pt, y_lo)), gf_ref[...])


def _moe_final(x, g, wr_hi, wr_lo, br, w1, w3, w2, g_final, tile, n_sub):
    n, d = x.shape
    n_e, _, f = w1.shape
    per_step = EXPERTS_PER_GROUP // n_sub
    tp = tile + LANES
    rows = tp + MOE_BLOCK
    row = pl.BlockSpec((tile, d), lambda i, s: (i, 0))
    cst = lambda shape: pl.BlockSpec(shape, lambda i, s: (0,) * len(shape))
    return pl.pallas_call(
        functools.partial(_moe_kernel, tile=tile, tp=tp, n_sub=n_sub),
        grid=(n // tile, N_GROUPS * n_sub),
        in_specs=[row, cst((1, d)), cst(wr_hi.shape), cst(wr_lo.shape), cst((1, LANES)),
                  pl.BlockSpec((per_step, d, f), lambda i, s: (s, 0, 0)),
                  pl.BlockSpec((per_step, d, f), lambda i, s: (s, 0, 0)),
                  pl.BlockSpec((per_step, f, d), lambda i, s: (s, 0, 0)),
                  cst((1, d))],
        out_specs=row,
        out_shape=jax.ShapeDtypeStruct((n, d), F32),
        scratch_shapes=[pltpu.VMEM((rows, d), BF16), pltpu.VMEM((rows, LANES), F32), pltpu.VMEM((rows, d), F32),
                        pltpu.VMEM((tile, tp), BF16), pltpu.SMEM((2, N_GROUPS), jnp.int32)],
        compiler_params=_cparams("arbitrary", "arbitrary"),
        name="moe_final",
    )(x, g.reshape(1, d), wr_hi, wr_lo, br, w1, w3, w2, g_final.reshape(1, d))


def _t5_bucket(rel):
    n = jnp.maximum(-rel, 0)
    max_exact = NUM_BUCKETS // 2
    nf = jnp.maximum(n, max_exact).astype(F32)
    large = max_exact + (jnp.log(nf / max_exact) / math.log(MAX_DISTANCE / max_exact)
                         * (NUM_BUCKETS - max_exact)).astype(jnp.int32)
    large = jnp.minimum(large, NUM_BUCKETS - 1)
    return jnp.where(n < max_exact, n, large)


def _distance_bias(rel_bias, n_dist):
    bucket = _t5_bucket(-jnp.arange(n_dist, dtype=jnp.int32))
    onehot = (bucket[:, None] == jnp.arange(NUM_BUCKETS, dtype=jnp.int32)[None, :]).astype(F32)
    return jnp.dot(onehot, rel_bias.astype(F32), precision=lax.Precision.HIGHEST).T


def _prompt_bias_tiles(rel_bias, s, tq):
    nd = s // tq
    p = 2 * tq
    table = _distance_bias(rel_bias, s)
    ext = jnp.pad(table, ((0, 0), (tq, 1)), constant_values=NEG_INF)
    c = jnp.stack([jnp.concatenate([ext[:, d * tq + 1:d * tq + tq + 1][:, ::-1],
                                    ext[:, d * tq + tq + 1:d * tq + p + 1][:, ::-1]], axis=1)
                   for d in range(nd)], axis=1)
    m = jnp.tile(c, (1, 1, tq))[:, :, :tq * (p - 1)].reshape(N_HEADS, nd, tq, p - 1)
    return m[..., :tq]


def _decode_bias(rel_bias, past_len, slots, n_new, page):
    table = _distance_bias(rel_bias, past_len + slots)
    past = jnp.stack([table[:, j + 1:j + 1 + past_len][:, ::-1] for j in range(slots)], axis=1)
    jq = jnp.arange(slots, dtype=jnp.int32)[:, None]
    jk = jnp.arange(page, dtype=jnp.int32)[None, :]
    visible = (jk <= jq) & (jk < n_new)
    new = jnp.where(visible[None], table[:, jnp.clip(jq - jk, 0, slots - 1)], NEG_INF)
    both = lambda t: jnp.broadcast_to(t[:, None], (N_HEADS, 2) + t.shape[1:]).reshape(N_HEADS * 2 * slots, -1)
    return both(past), both(new)


def _head_queries(q, n_q, slots):
    nb = q.shape[0] // n_q
    qh = jnp.transpose(q.reshape(nb, n_q, N_HEADS, V_DIM), (0, 2, 1, 3))
    qh = jnp.pad(qh, ((0, 0), (0, 0), (0, slots - n_q), (0, 0)))
    first = jnp.arange(V_DIM) < HEAD_DIM
    zero = jnp.zeros((), q.dtype)
    return jnp.concatenate([jnp.where(first, qh, zero), jnp.where(first, zero, qh)], axis=2)


def _head_major(t, n_q, slots):
    nb = t.shape[0] // n_q
    th = jnp.transpose(t.reshape(nb, n_q, N_HEADS, V_DIM), (0, 2, 1, 3))
    return jnp.pad(th, ((0, 0), (0, 0), (0, slots - n_q), (0, 0)))


def _pick_tile(n, pref):
    return pref if n % pref == 0 else n


def kernel(x_prompt, x_sample, mem_prompt, cache_k, cache_v, page_table, cache_mem_k, cache_mem_v, state_pool, rel_bias, norm_mix_g, w_in, lambda_q1, lambda_k1, lambda_q2, lambda_k2, subln_g, w_pool_grp, pool_scale, w_br_attn, w_br_pool, w_gate, w_out, norm_cross_g, w_cq, w_ck, w_cv, w_co, norm_ffn_g, w_router_grp, b_router_grp, w_router_exp, b_router_exp, w_exp_gate, w_exp_up, w_exp_down, norm_final_g):
    depth = w_in.shape[0]
    assert depth == 1, "single-layer step"
    l = 0
    b_p, s_p, d = x_prompt.shape
    b_s, s_s, _ = x_sample.shape
    n_p, n_s = b_p * s_p, b_s * s_s
    n_mem = mem_prompt.shape[1]
    page = cache_k.shape[2]
    past_len = page_table.shape[1] * page
    pool_w = state_pool.shape[-1]

    lam_init = 0.8 - 0.6 * math.exp(-0.3 * l)
    lam = (jnp.exp(jnp.sum(lambda_q1[l].astype(F32) * lambda_k1[l].astype(F32)))
           - jnp.exp(jnp.sum(lambda_q2[l].astype(F32) * lambda_k2[l].astype(F32))) + lam_init).reshape(1, 1)

    bf = lambda w: w.astype(BF16)
    w_in_bf, w_gate_bf, w_ba_bf, w_bp_bf, w_out_bf = bf(w_in[l]), bf(w_gate[l]), bf(w_br_attn[l]), bf(w_br_pool[l]), bf(w_out[l])
    w_cq_bf, w_ck_bf, w_cv_bf, w_co_bf = bf(w_cq[l]), bf(w_ck[l]), bf(w_cv[l]), bf(w_co[l])
    w_grp_bf = bf(w_pool_grp[l])
    w1_bf, w3_bf, w2_bf = bf(w_exp_gate[l]), bf(w_exp_up[l]), bf(w_exp_down[l])
    wr = jnp.concatenate([w_router_exp[l], w_router_grp[l]], axis=1).astype(F32)
    wr = jnp.pad(wr, ((0, 0), (0, LANES - wr.shape[1])))
    wr_hi = wr.astype(BF16)
    wr_lo = (wr - wr_hi.astype(F32)).astype(BF16)
    br = jnp.concatenate([b_router_exp[l], b_router_grp[l]]).astype(F32)
    br = jnp.pad(br, (0, LANES - br.shape[0])).reshape(1, LANES)

    def token_tail(x_tok, a, pooled, mem_k, mem_v, nb, tile, cross_tile):
        x1 = _merge(x_tok, a, pooled, norm_mix_g[l], w_gate_bf, w_ba_bf, w_bp_bf, w_out_bf, tile)
        x1 = x1.reshape(nb, -1, d)
        seq = x1.shape[1]
        x1 = jnp.pad(x1, ((0, 0), (0, -seq % BF16_ROWS), (0, 0)))
        x2 = _cross(x1, mem_k, mem_v, norm_cross_g[l], w_cq_bf, w_co_bf, max(cross_tile, BF16_ROWS))[:, :seq]
        n_tok = x_tok.shape[0]
        return _moe_final(x2.reshape(-1, d), norm_ffn_g[l], wr_hi, wr_lo, br, w1_bf, w3_bf, w2_bf,
                          norm_final_g, _pick_tile(n_tok, 1024), 2)

    tile_p = _pick_tile(n_p, 512)
    tq = _pick_tile(s_p, 256)
    q, k, v, u = _inproj(x_prompt.reshape(n_p, d), norm_mix_g[l], w_in_bf, tile_p)
    bias_tiles = _prompt_bias_tiles(rel_bias, s_p, tq)
    a = _prompt_attention(lam, q.reshape(b_p, s_p, d), k.reshape(b_p, s_p, d), v.reshape(b_p, s_p, d),
                          bias_tiles, subln_g[l], lam_init, tq)
    u3 = u.reshape(b_p, s_p, pool_w)
    pooled = _pool_mix(u3, w_grp_bf, pool_scale[l], _pick_tile(s_p, 256))
    mem_k, mem_v = _memkv(mem_prompt.reshape(b_p * n_mem, d), w_ck_bf, w_cv_bf, _pick_tile(b_p * n_mem, 512))
    mem_k, mem_v = mem_k.reshape(b_p, n_mem, d), mem_v.reshape(b_p, n_mem, d)
    y_prompt = token_tail(x_prompt.reshape(n_p, d), a.reshape(n_p, d), pooled.reshape(n_p, pool_w),
                          mem_k, mem_v, b_p, tile_p, _pick_tile(s_p, 512)).reshape(b_p, s_p, d)
    k_prompt = k.reshape(1, b_p, s_p, N_HEADS, V_DIM)
    v_prompt = v.reshape(1, b_p, s_p, N_HEADS, V_DIM)
    pool_prompt = u3[:, -POOL_CTX:][None]
    x_heads_dim = d // X_HEADS
    mem_k_prompt = mem_k.reshape(1, b_p, n_mem, X_HEADS, x_heads_dim)
    mem_v_prompt = mem_v.reshape(1, b_p, n_mem, X_HEADS, x_heads_dim)

    qs, ks, vs, us = _inproj(x_sample.reshape(n_s, d), norm_mix_g[l], w_in_bf, n_s)
    slots = -(-s_s // 8) * 8
    bias_past, bias_new = _decode_bias(rel_bias, past_len, slots, s_s, page)
    paged = lambda c: c.reshape(-1, page * N_HEADS, V_DIM)
    a_s = _decode_attention(page_table.astype(jnp.int32), lam, _head_queries(qs, s_s, slots),
                            paged(cache_k), paged(cache_v), bias_past,
                            _head_major(ks, s_s, slots), _head_major(vs, s_s, slots), bias_new,
                            subln_g[l], lam_init, page)[:, :s_s]
    ctx = state_pool[l].astype(F32)
    full = jnp.concatenate([ctx, us.reshape(b_s, s_s, pool_w)], axis=1)
    seq_pad = -full.shape[1] % 32
    pooled_s = _pool_mix(jnp.pad(full, ((0, 0), (0, seq_pad), (0, 0))), w_grp_bf, pool_scale[l],
                         full.shape[1] + seq_pad)[:, POOL_CTX:POOL_CTX + s_s]
    y_sample = token_tail(x_sample.reshape(n_s, d), a_s.reshape(n_s, d).astype(BF16), pooled_s.reshape(n_s, pool_w),
                          cache_mem_k[l].reshape(b_s, n_mem, d), cache_mem_v[l].reshape(b_s, n_mem, d),
                          b_s, n_s, s_s).reshape(b_s, s_s, d)
    k_sample = ks.reshape(1, b_s, s_s, N_HEADS, V_DIM)
    v_sample = vs.reshape(1, b_s, s_s, N_HEADS, V_DIM)
    pool_sample = full[:, -POOL_CTX:][None]

    return (y_prompt, y_sample, k_prompt, v_prompt, pool_prompt, mem_k_prompt, mem_v_prompt,
            k_sample, v_sample, pool_sample)
```

```python
import functools
import math

import jax
import jax.numpy as jnp
from jax import lax
from jax.experimental import pallas as pl
from jax.experimental.pallas import tpu as pltpu

F32 = jnp.float32
BF16 = jnp.bfloat16

RMS_EPS = 1e-6
N_HEADS = 8
HEAD_DIM = 64
V_DIM = 2 * HEAD_DIM
NUM_BUCKETS = 32
MAX_DISTANCE = 128
POOL_WINDOWS = (2, 4, 8, 16)
POOL_GROUP_DIM = 128
POOL_CTX = max(POOL_WINDOWS) - 1
X_HEADS = 4
N_GROUPS = 4
EXPERTS_PER_GROUP = 8
N_EXPERTS = N_GROUPS * EXPERTS_PER_GROUP
LANES = 128
BF16_ROWS = 16
VMEM_LIMIT = 56 * 1024 * 1024
PAGES_PER_STEP = 8
MOE_BLOCK = 288
NEG_INF = float("-inf")


def _cparams(*sem):
    return pltpu.CompilerParams(dimension_semantics=sem, vmem_limit_bytes=VMEM_LIMIT)


def _rms(x, g):
    return x * lax.rsqrt(jnp.mean(x * x, axis=-1, keepdims=True) + RMS_EPS) * g


def _dot(a, b):
    return jnp.dot(a, b, preferred_element_type=F32)


def _dot_nt(a, b):
    return lax.dot_general(a, b, (((1,), (1,)), ((), ())), preferred_element_type=F32)


def _const_spec(shape):
    zeros = (0,) * len(shape)
    return pl.BlockSpec(shape, lambda *_: zeros)


POOL_HALO = 16
POOL_SUB = 256


def _split_bf16(x):
    hi = x.astype(BF16)
    return hi, (x - hi.astype(F32)).astype(BF16)


def _pool_rows(u, prev, pos0, bc_ref, bp_ref, w_ref, scale_ref, o_ref):
    sub = bc_ref.shape[1]
    for r0 in range(0, u.shape[0], sub):
        us = u[r0:r0 + sub]
        u_hi, u_lo = _split_bf16(us)
        p_hi, p_lo = _split_bf16(prev if r0 == 0 else u[r0 - POOL_HALO:r0])
        pos = pos0 + r0 + lax.broadcasted_iota(jnp.int32, (sub, 1), 0)
        for g, w in enumerate(POOL_WINDOWS):
            sl = slice(g * POOL_GROUP_DIM, (g + 1) * POOL_GROUP_DIM)
            bc, bp = bc_ref[g], bp_ref[g]
            tot = (_dot(bc, u_hi[:, sl]) + _dot(bc, u_lo[:, sl])) + (_dot(bp, p_hi[:, sl]) + _dot(bp, p_lo[:, sl]))
            cnt = jnp.minimum(pos + 1, w).astype(F32)
            pooled = tot / cnt - us[:, sl]
            mixed = _dot(pooled.astype(BF16), w_ref[g]) * scale_ref[:, sl]
            o_ref[r0:r0 + sub, sl] = mixed.astype(o_ref.dtype)


def _pool_bands(tile):
    r = jnp.arange(tile, dtype=jnp.int32)[:, None]
    c = jnp.arange(tile, dtype=jnp.int32)[None, :]
    ch = jnp.arange(POOL_HALO, dtype=jnp.int32)[None, :] - POOL_HALO
    bc = jnp.stack([((r - c >= 0) & (r - c < w)) for w in POOL_WINDOWS]).astype(BF16)
    bp = jnp.stack([(r - ch < w) for w in POOL_WINDOWS]).astype(BF16)
    return bc, bp


def _inproj_kernel(x_ref, g_ref, w_ref, q_ref, k_ref, v_ref, u_ref, *, d):
    h = _rms(x_ref[...], g_ref[...]).astype(BF16)
    q_ref[...] = (_dot(h, w_ref[:, 0:d]) * (HEAD_DIM ** -0.5)).astype(BF16)
    k_ref[...] = _dot(h, w_ref[:, d:2 * d])
    v_ref[...] = _dot(h, w_ref[:, 2 * d:3 * d])
    u_ref[...] = _dot(h, w_ref[:, 3 * d:])


def _inproj(x, g, w_bf, tile):
    n, d = x.shape
    wu = w_bf.shape[1] - 3 * d
    row = lambda w: pl.BlockSpec((tile, w), lambda i: (i, 0))
    return pl.pallas_call(
        functools.partial(_inproj_kernel, d=d),
        grid=(n // tile,),
        in_specs=[row(d), _const_spec((1, d)), _const_spec(w_bf.shape)],
        out_specs=[row(d), row(d), row(d), row(wu)],
        out_shape=[jax.ShapeDtypeStruct((n, d), BF16), jax.ShapeDtypeStruct((n, d), F32),
                   jax.ShapeDtypeStruct((n, d), F32), jax.ShapeDtypeStruct((n, wu), F32)],
        compiler_params=_cparams("arbitrary"),
        name="inproj",
    )(x, g.reshape(1, d), w_bf)


def _head_norm(o, g, lam_init):
    return _rms(o, g) * (1.0 - lam_init)


def _lane_chunks(x):
    return [x[:, c * LANES:(c + 1) * LANES] for c in range(x.shape[1] // LANES)]


N_NEAR = 2


def _pattn_kernel(lam_ref, q_ref, k_ref, v_ref, bias_ref, g_ref, o_ref, kb_ref, vb_ref, *, tq, nq, lam_init):
    j_id = pl.program_id(2)

    @pl.when(j_id == 0)
    def _():
        kb_ref[...] = k_ref[...].astype(BF16)
        vb_ref[...] = v_ref[...].astype(BF16)

    def block(qi):
        n = qi + 1
        rows = slice(qi * tq, (qi + 1) * tq)
        q = q_ref[rows, :]
        lane = lax.broadcasted_iota(jnp.int32, q.shape, 1)
        zero = jnp.zeros_like(q)
        n_far = max(n - N_NEAR, 0)
        c_far = bias_ref[N_NEAR][0:1, 0:1]
        outs = []
        for qm in (jnp.where(lane < HEAD_DIM, q, zero), jnp.where(lane >= HEAD_DIM, q, zero)):
            s = [_dot_nt(qm, kb_ref[kb * tq:(kb + 1) * tq, :]) for kb in range(n)]
            s = [t if kb < n_far else t + bias_ref[n - 1 - kb] for kb, t in enumerate(s)]
            mx = functools.reduce(jnp.maximum, [c for t in s[n_far:] for c in _lane_chunks(t)])
            if n_far:
                mx = jnp.maximum(mx, functools.reduce(jnp.maximum, [c for t in s[:n_far] for c in _lane_chunks(t)])
                                 + c_far)
            m_near = jnp.broadcast_to(jnp.max(mx, axis=1, keepdims=True), (tq, LANES))
            m_far = m_near - c_far
            ls, pv = None, None
            for kb, t in enumerate(s):
                mb = m_far if kb < n_far else m_near
                p = [jnp.exp(c - mb) for c in _lane_chunks(t)]
                t_pv = _dot(jnp.concatenate(p, axis=1).astype(BF16), vb_ref[kb * tq:(kb + 1) * tq, :])
                t_ls = functools.reduce(jnp.add, p)
                ls, pv = (t_ls, t_pv) if ls is None else (ls + t_ls, pv + t_pv)
            outs.append(pv / jnp.sum(ls, axis=1, keepdims=True))
        o = outs[0] - lam_ref[0, 0] * outs[1]
        o_ref[rows, :] = _head_norm(o, g_ref[...], lam_init).astype(o_ref.dtype)

    def pair(j):
        block(j)
        if nq - 1 - j != j:
            block(nq - 1 - j)

    for j in range((nq + 1) // 2):
        pl.when(j_id == j)(functools.partial(pair, j))


def _prompt_attention(lam, q, k, v, bias_tiles, subln_g, lam_init, tq):
    b, s, d = q.shape
    nq = s // tq
    assert tq + 1 >= MAX_DISTANCE and bias_tiles.shape[1] == N_NEAR + 1
    seq = pl.BlockSpec((None, s, V_DIM), lambda h, bi, j: (bi, 0, h))
    return pl.pallas_call(
        functools.partial(_pattn_kernel, tq=tq, nq=nq, lam_init=lam_init),
        grid=(N_HEADS, b, (nq + 1) // 2),
        in_specs=[pl.BlockSpec(memory_space=pltpu.SMEM), seq, seq, seq,
                  pl.BlockSpec((None, N_NEAR + 1, tq, tq), lambda h, bi, j: (h, 0, 0, 0)),
                  _const_spec((1, V_DIM))],
        out_specs=seq,
        out_shape=jax.ShapeDtypeStruct((b, s, d), BF16),
        scratch_shapes=[pltpu.VMEM((s, V_DIM), BF16), pltpu.VMEM((s, V_DIM), BF16)],
        compiler_params=_cparams("arbitrary", "arbitrary", "arbitrary"),
        name="prompt_attn",
    )(lam, q, k, v, bias_tiles, subln_g.reshape(1, V_DIM))


def _dattn_kernel(pt_ref, lam_ref, q_ref, *refs, page, lam_init):
    del pt_ref
    npg = PAGES_PER_STEP
    k_refs, v_refs = refs[:npg], refs[npg:2 * npg]
    bias_ref, kn_ref, vn_ref, biasn_ref, g_ref, o_ref, m_ref, l_ref, acc_ref = refs[2 * npg:]
    g_idx, b = pl.program_id(0), pl.program_id(1)
    hr = m_ref.shape[1] // N_HEADS
    half = hr // 2

    @pl.when(g_idx == 0)
    def _():
        m_ref[b] = jnp.full(m_ref.shape[1:], NEG_INF, F32)
        l_ref[b] = jnp.zeros(l_ref.shape[1:], F32)
        acc_ref[b] = jnp.zeros(acc_ref.shape[1:], F32)

    def head_rows(ref, h):
        return ref[pl.ds(h, page, stride=N_HEADS), :].astype(BF16)

    def update(s, v_blocks):
        m_old = m_ref[b]
        m_new = jnp.maximum(m_old, jnp.max(s, axis=1, keepdims=True))
        alpha = jnp.exp(m_old - m_new)
        p = jnp.exp(s - m_new)
        l_ref[b] = alpha * l_ref[b] + jnp.sum(p, axis=1, keepdims=True)
        m_ref[b] = m_new
        pb = p.astype(BF16)
        for h in range(N_HEADS):
            rs = slice(h * hr, (h + 1) * hr)
            pv = None
            for i, v in enumerate(v_blocks[h]):
                t = _dot(pb[rs, i * page:(i + 1) * page], v)
                pv = t if pv is None else pv + t
            acc_ref[b, h] = alpha[rs] * acc_ref[b, h] + pv

    def page_scores(k_of_head):
        acc = None
        for pr in range(N_HEADS // 2):
            keys = jnp.concatenate([k_of_head(2 * pr), k_of_head(2 * pr + 1)], axis=1)
            t = _dot(keys, q_ref[pr])
            acc = t if acc is None else acc + t
        return acc.T

    s = jnp.concatenate([page_scores(lambda h, kr=kr: head_rows(kr, h)) for kr in k_refs], axis=1)
    update(s + bias_ref[...], [[head_rows(vr, h) for vr in v_refs] for h in range(N_HEADS)])

    @pl.when(g_idx == pl.num_programs(0) - 1)
    def _():
        def pad_page(x):
            return jnp.concatenate([x, jnp.zeros((page - x.shape[0], x.shape[1]), x.dtype)], axis=0).astype(BF16)

        sn = page_scores(lambda h: pad_page(kn_ref[h]))
        update(sn + biasn_ref[...], [[pad_page(vn_ref[h])] for h in range(N_HEADS)])
        linv = 1.0 / l_ref[b]
        g = g_ref[...]
        for h in range(N_HEADS):
            o = acc_ref[b, h] * linv[h * hr:(h + 1) * hr]
            o = o[:half] - lam_ref[0, 0] * o[half:]
            o_ref[b, :, h * V_DIM:(h + 1) * V_DIM] = _head_norm(o, g, lam_init)


def _decode_attention(page_table, lam, q_heads, cache_k, cache_v, bias_past, k_new, v_new, bias_new,
                      subln_g, lam_init, page):
    nb, n_pages = page_table.shape
    npg = PAGES_PER_STEP
    rows = q_heads.shape[-1]
    hr = rows // N_HEADS
    slots = hr // 2
    d = N_HEADS * V_DIM
    assert rows == LANES

    def page_spec(i):
        return pl.BlockSpec((None, page * N_HEADS, V_DIM), lambda g, b, pt: (pt[b, g * npg + i], 0, 0))

    seq = lambda *blk: pl.BlockSpec((None,) + blk, lambda g, b, pt: (b,) + (0,) * len(blk))
    in_specs = ([pl.BlockSpec(memory_space=pltpu.SMEM), seq(N_HEADS // 2, 2 * V_DIM, rows)]
                + [page_spec(i) for i in range(npg)] * 2
                + [pl.BlockSpec((rows, npg * page), lambda g, b, pt: (0, g)),
                   seq(N_HEADS, slots, V_DIM), seq(N_HEADS, slots, V_DIM),
                   pl.BlockSpec((rows, page), lambda g, b, pt: (0, 0)),
                   pl.BlockSpec((1, V_DIM), lambda g, b, pt: (0, 0))])
    return pl.pallas_call(
        functools.partial(_dattn_kernel, page=page, lam_init=lam_init),
        grid_spec=pltpu.PrefetchScalarGridSpec(
            num_scalar_prefetch=1,
            grid=(n_pages // npg, nb),
            in_specs=in_specs,
            out_specs=pl.BlockSpec((nb, slots, d), lambda g, b, pt: (0, 0, 0)),
            scratch_shapes=[pltpu.VMEM((nb, rows, 1), F32), pltpu.VMEM((nb, rows, 1), F32),
                            pltpu.VMEM((nb, N_HEADS, hr, V_DIM), F32)]),
        out_shape=jax.ShapeDtypeStruct((nb, slots, d), F32),
        compiler_params=_cparams("arbitrary", "arbitrary"),
        name="decode_attn",
    )(page_table, lam, q_heads, *([cache_k] * npg), *([cache_v] * npg), bias_past, k_new, v_new, bias_new,
      subln_g.reshape(1, V_DIM))


def _pool_kernel(u_ref, prev_ref, bc_ref, bp_ref, w_ref, scale_ref, o_ref, *, tile):
    i = pl.program_id(1)
    prev = jnp.where(i > 0, prev_ref[...], 0.0)
    _pool_rows(u_ref[...], prev, i * tile, bc_ref, bp_ref, w_ref, scale_ref, o_ref)


def _pool_mix(u, w_grp_bf, scale, tile):
    b, s, wdt = u.shape
    bc, bp = _pool_bands(min(tile, POOL_SUB))
    assert tile % bc.shape[1] == 0
    per_halo = tile // POOL_HALO
    return pl.pallas_call(
        functools.partial(_pool_kernel, tile=tile),
        grid=(b, s // tile),
        in_specs=[pl.BlockSpec((None, tile, wdt), lambda bi, i: (bi, i, 0)),
                  pl.BlockSpec((None, POOL_HALO, wdt), lambda bi, i: (bi, jnp.maximum(i * per_halo - 1, 0), 0)),
                  _const_spec(bc.shape), _const_spec(bp.shape), _const_spec(w_grp_bf.shape),
                  _const_spec((1, wdt))],
        out_specs=pl.BlockSpec((None, tile, wdt), lambda bi, i: (bi, i, 0)),
        out_shape=jax.ShapeDtypeStruct((b, s, wdt), BF16),
        compiler_params=_cparams("arbitrary", "arbitrary"),
        name="pool_mix",
    )(u, u, bc, bp, w_grp_bf, scale.reshape(1, wdt))


def _merge_kernel(x_ref, a_ref, p_ref, g_ref, wg_ref, wa_ref, wp_ref, wo_ref, o_ref, *, d):
    x = x_ref[...]
    h = _rms(x, g_ref[...]).astype(BF16)
    ga = jax.nn.sigmoid(_dot(h, wg_ref[:, :d]))
    merged = ga * _dot(a_ref[...], wa_ref[...])
    gp = jax.nn.sigmoid(_dot(h, wg_ref[:, d:]))
    merged += gp * _dot(p_ref[...], wp_ref[...])
    o_ref[...] = x + _dot(merged.astype(BF16), wo_ref[...])


def _merge(x, a, pooled, g, w_gate, w_br_attn, w_br_pool, w_out, tile):
    n, d = x.shape
    row = lambda w: pl.BlockSpec((tile, w), lambda i: (i, 0))
    return pl.pallas_call(
        functools.partial(_merge_kernel, d=d),
        grid=(n // tile,),
        in_specs=[row(d), row(a.shape[1]), row(pooled.shape[1]), _const_spec((1, d)),
                  _const_spec(w_gate.shape), _const_spec(w_br_attn.shape), _const_spec(w_br_pool.shape),
                  _const_spec(w_out.shape)],
        out_specs=row(d),
        out_shape=jax.ShapeDtypeStruct((n, d), F32),
        compiler_params=_cparams("arbitrary"),
        name="merge",
    )(x, a, pooled, g.reshape(1, d), w_gate, w_br_attn, w_br_pool, w_out)


def _memkv_kernel(x_ref, wk_ref, wv_ref, k_ref, v_ref):
    x = x_ref[...].astype(BF16)
    k_ref[...] = _dot(x, wk_ref[...])
    v_ref[...] = _dot(x, wv_ref[...])


def _memkv(x, wk, wv, tile):
    n, d = x.shape
    row = pl.BlockSpec((tile, d), lambda i: (i, 0))
    return pl.pallas_call(
        _memkv_kernel,
        grid=(n // tile,),
        in_specs=[row, _const_spec(wk.shape), _const_spec(wv.shape)],
        out_specs=[row, row],
        out_shape=[jax.ShapeDtypeStruct((n, d), F32)] * 2,
        compiler_params=_cparams("arbitrary"),
        name="mem_kv",
    )(x, wk, wv)


def _cross_kernel(x_ref, mk_ref, mv_ref, g_ref, wq_ref, wo_ref, o_ref, *, dh):
    x = x_ref[...]
    h = _rms(x, g_ref[...]).astype(BF16)
    q = (_dot(h, wq_ref[...]) * (dh ** -0.5)).astype(BF16)
    out = x
    for hh in range(X_HEADS):
        sl = slice(hh * dh, (hh + 1) * dh)
        s = _dot_nt(q[:, sl], mk_ref[:, sl].astype(BF16))
        p = jnp.exp(s - jnp.max(s, axis=1, keepdims=True))
        o = _dot(p.astype(BF16), mv_ref[:, sl].astype(BF16)) / jnp.sum(p, axis=1, keepdims=True)
        out += _dot(o.astype(BF16), wo_ref[sl, :])
    o_ref[...] = out


def _cross(x, mem_k, mem_v, g, w_cq, w_co, tile):
    b, s, d = x.shape
    m = mem_k.shape[1]
    xs = pl.BlockSpec((None, tile, d), lambda bi, i: (bi, i, 0))
    ms = pl.BlockSpec((None, m, d), lambda bi, i: (bi, 0, 0))
    return pl.pallas_call(
        functools.partial(_cross_kernel, dh=d // X_HEADS),
        grid=(b, s // tile),
        in_specs=[xs, ms, ms, _const_spec((1, d)), _const_spec(w_cq.shape), _const_spec(w_co.shape)],
        out_specs=xs,
        out_shape=jax.ShapeDtypeStruct((b, s, d), F32),
        compiler_params=_cparams("arbitrary", "arbitrary"),
        name="cross_attn",
    )(x, mem_k, mem_v, g.reshape(1, d), w_cq, w_co)


def _router_gates(h, wr_hi, wr_lo, br):
    h_hi, h_lo = _split_bf16(h)
    logits = _dot(h_hi, wr_hi) + (_dot(h_lo, wr_hi) + _dot(h_hi, wr_lo)) + br
    lane = lax.broadcasted_iota(jnp.int32, logits.shape, 1)
    big = jnp.int32(LANES)
    first = lambda mask: jnp.min(jnp.where(mask, lane, big), axis=1, keepdims=True)
    rmax = lambda x: jnp.max(x, axis=1, keepdims=True)

    gmask = (lane >= N_EXPERTS) & (lane < N_EXPERTS + N_GROUPS)
    gl = jnp.where(gmask, logits, NEG_INF)
    gmax = rmax(gl)
    grp = first(gl == gmax) - N_EXPERTS
    p_grp = 1.0 / jnp.sum(jnp.exp(gl - gmax), axis=1, keepdims=True)

    lo = grp * EXPERTS_PER_GROUP
    el = jnp.where((lane >= lo) & (lane < lo + EXPERTS_PER_GROUP), logits, NEG_INF)
    v1 = rmax(el)
    i1 = first(el == v1)
    el2 = jnp.where(lane == i1, NEG_INF, el)
    v2 = rmax(el2)
    i2 = first(el2 == v2)
    t = jnp.exp(v2 - v1)
    w1 = p_grp / (1.0 + t)
    return jnp.where(lane == i1, w1, 0.0) + jnp.where(lane == i2, w1 * t, 0.0), grp


def _split3_bf16(x):
    a = x.astype(BF16)
    r = x - a.astype(F32)
    b = r.astype(BF16)
    return a, b, (r - b.astype(F32)).astype(BF16)


def _moe_kernel(x_ref, g_ref, wrh_ref, wrl_ref, br_ref, w1_ref, w3_ref, w2_ref, gf_ref, o_ref,
                hs_ref, gs_ref, ys_ref, pt_ref, meta_ref, *, tile, tp, n_sub):
    sg = pl.program_id(1)
    per_step = EXPERTS_PER_GROUP // n_sub

    @pl.when(sg == 0)
    def _route():
        h = _rms(x_ref[...], g_ref[...])
        gates, grp = _router_gates(h, wrh_ref[...], wrl_ref[...], br_ref[...])
        lane = lax.broadcasted_iota(jnp.int32, gates.shape, 1)
        member = lane == grp
        r = lax.broadcasted_iota(jnp.int32, (tile, tile), 0)
        c = lax.broadcasted_iota(jnp.int32, (tile, tile), 1)
        tri = jnp.where(r >= c, 1.0, 0.0).astype(BF16)
        csum = _dot(tri, jnp.where(member, 1.0, 0.0).astype(BF16))
        cnt = csum[tile - 1:tile, :]
        padded = jnp.ceil(cnt * (1.0 / BF16_ROWS)) * BF16_ROWS
        lane_row = lane[0:1, :]
        off = jnp.zeros_like(cnt)
        run = jnp.zeros((1, 1), F32)
        for gg in range(N_GROUPS):
            off = jnp.where(lane_row == gg, run, off)
            run = run + padded[:, gg:gg + 1]
        pos = jnp.sum(jnp.where(member, off + csum - 1.0, 0.0), axis=1, keepdims=True)
        pos_row = jnp.transpose(jnp.broadcast_to(pos, (tile, LANES)))[0:1, :]
        pt = jnp.where(lax.broadcasted_iota(jnp.int32, (tile, tp), 1) == pos.astype(jnp.int32), 1.0, 0.0)
        p = jnp.where(lax.broadcasted_iota(jnp.int32, (tp, tile), 0) == pos_row.astype(jnp.int32), 1.0, 0.0)
        p = p.astype(BF16)
        pt_ref[...] = pt.astype(BF16)
        hs_ref[0:tp, :] = _dot(p, h.astype(BF16)).astype(BF16)
        hs_ref[tp:, :] = jnp.zeros((hs_ref.shape[0] - tp, hs_ref.shape[1]), BF16)
        grel = gates
        for gg in range(1, N_GROUPS):
            grel = jnp.where(grp == gg, pltpu.roll(gates, LANES - gg * EXPERTS_PER_GROUP, axis=1), grel)
        gs_ref[0:tp, :] = functools.reduce(jnp.add, [_dot(p, t) for t in _split3_bf16(grel)])
        gs_ref[tp:, :] = jnp.zeros((gs_ref.shape[0] - tp, gs_ref.shape[1]), F32)
        ys_ref[...] = jnp.zeros(ys_ref.shape, F32)
        for gg in range(N_GROUPS):
            pick = lambda v: jnp.sum(jnp.where(lane_row == gg, v, 0.0)).astype(jnp.int32)
            meta_ref[0, gg] = pick(off)
            meta_ref[1, gg] = pick(cnt)

    grp_id = sg // n_sub
    e0 = (sg % n_sub) * per_step
    start = meta_ref[0, grp_id]
    cnt = meta_ref[1, grp_id]
    lane8 = lax.broadcasted_iota(jnp.int32, (MOE_BLOCK, LANES), 1)

    def block(i, carry):
        r0 = pl.multiple_of(start + i * MOE_BLOCK, BF16_ROWS)
        rows = hs_ref[pl.ds(r0, MOE_BLOCK), :]
        ridx = r0 + lax.broadcasted_iota(jnp.int32, (MOE_BLOCK, 1), 0)
        gts = jnp.where(ridx < start + cnt, gs_ref[pl.ds(r0, MOE_BLOCK), :], 0.0)
        y = None
        for j in range(per_step):
            gate = jnp.sum(jnp.where(lane8 == e0 + j, gts, 0.0), axis=1, keepdims=True)
            h1 = _dot(rows, w1_ref[j])
            h3 = _dot(rows, w3_ref[j])
            act = (h1 * jax.nn.sigmoid(h1)) * h3 * gate
            t = _dot(act.astype(BF16), w2_ref[j])
            y = t if y is None else y + t
        ys_ref[pl.ds(r0, MOE_BLOCK), :] += y
        return carry

    lax.fori_loop(0, (cnt + MOE_BLOCK - 1) // MOE_BLOCK, block, 0)

    @pl.when(sg == pl.num_programs(1) - 1)
    def _finish():
        pt = pt_ref[...]
        y_hi, y_lo = _split_bf16(ys_ref[0:tp, :])
        o_ref[...] = _rms(x_ref[...] + (_dot(pt, y_hi) + _dot(pt, y_lo)), gf_ref[...])


def _moe_final(x, g, wr_hi, wr_lo, br, w1, w3, w2, g_final, tile, n_sub):
    n, d = x.shape
    n_e, _, f = w1.shape
    per_step = EXPERTS_PER_GROUP // n_sub
    tp = tile + LANES
    rows = tp + MOE_BLOCK
    row = pl.BlockSpec((tile, d), lambda i, s: (i, 0))
    cst = lambda shape: pl.BlockSpec(shape, lambda i, s: (0,) * len(shape))
    return pl.pallas_call(
        functools.partial(_moe_kernel, tile=tile, tp=tp, n_sub=n_sub),
        grid=(n // tile, N_GROUPS * n_sub),
        in_specs=[row, cst((1, d)), cst(wr_hi.shape), cst(wr_lo.shape), cst((1, LANES)),
                  pl.BlockSpec((per_step, d, f), lambda i, s: (s, 0, 0)),
                  pl.BlockSpec((per_step, d, f), lambda i, s: (s, 0, 0)),
                  pl.BlockSpec((per_step, f, d), lambda i, s: (s, 0, 0)),
                  cst((1, d))],
        out_specs=row,
        out_shape=jax.ShapeDtypeStruct((n, d), F32),
        scratch_shapes=[pltpu.VMEM((rows, d), BF16), pltpu.VMEM((rows, LANES), F32), pltpu.VMEM((rows, d), F32),
                        pltpu.VMEM((tile, tp), BF16), pltpu.SMEM((2, N_GROUPS), jnp.int32)],
        compiler_params=_cparams("arbitrary", "arbitrary"),
        name="moe_final",
    )(x, g.reshape(1, d), wr_hi, wr_lo, br, w1, w3, w2, g_final.reshape(1, d))


def _t5_bucket(rel):
    n = jnp.maximum(-rel, 0)
    max_exact = NUM_BUCKETS // 2
    nf = jnp.maximum(n, max_exact).astype(F32)
    large = max_exact + (jnp.log(nf / max_exact) / math.log(MAX_DISTANCE / max_exact)
                         * (NUM_BUCKETS - max_exact)).astype(jnp.int32)
    large = jnp.minimum(large, NUM_BUCKETS - 1)
    return jnp.where(n < max_exact, n, large)


def _distance_bias(rel_bias, n_dist, descending=False):
    dist = jnp.arange(n_dist, dtype=jnp.int32)
    bucket = _t5_bucket(-(n_dist - 1 - dist if descending else dist))
    onehot = (bucket[:, None] == jnp.arange(NUM_BUCKETS, dtype=jnp.int32)[None, :]).astype(F32)
    return jnp.dot(onehot, rel_bias.astype(F32), precision=lax.Precision.HIGHEST).T


def _prompt_bias_tiles(rel_bias, tq, nd):
    p = 2 * tq
    table = _distance_bias(rel_bias, nd * tq)
    ext = jnp.pad(table, ((0, 0), (tq, 1)), constant_values=NEG_INF)
    c = jnp.stack([jnp.concatenate([ext[:, d * tq + 1:d * tq + tq + 1][:, ::-1],
                                    ext[:, d * tq + tq + 1:d * tq + p + 1][:, ::-1]], axis=1)
                   for d in range(nd)], axis=1)
    m = jnp.tile(c, (1, 1, tq))[:, :, :tq * (p - 1)].reshape(N_HEADS, nd, tq, p - 1)
    return m[..., :tq]


def _decode_bias(rel_bias, past_len, slots, n_new, page):
    n_dist = past_len + slots
    table = _distance_bias(rel_bias, n_dist, descending=True)
    past = jnp.stack([table[:, slots - 1 - j:slots - 1 - j + past_len] for j in range(slots)], axis=1)
    jq = jnp.arange(slots, dtype=jnp.int32)[:, None]
    jk = jnp.arange(page, dtype=jnp.int32)[None, :]
    visible = (jk <= jq) & (jk < n_new)
    near = table[:, n_dist - slots:][:, ::-1]
    new = jnp.where(visible[None], near[:, jnp.clip(jq - jk, 0, slots - 1)], NEG_INF)
    both = lambda t: jnp.broadcast_to(t[:, None], (N_HEADS, 2) + t.shape[1:]).reshape(N_HEADS * 2 * slots, -1)
    return both(past), both(new)


def _head_queries(q, n_q, slots):
    nb = q.shape[0] // n_q
    qh = jnp.transpose(q.reshape(nb, n_q, N_HEADS, V_DIM), (0, 2, 1, 3))
    qh = jnp.pad(qh, ((0, 0), (0, 0), (0, slots - n_q), (0, 0)))
    first = jnp.arange(V_DIM) < HEAD_DIM
    zero = jnp.zeros((), q.dtype)
    return jnp.concatenate([jnp.where(first, qh, zero), jnp.where(first, zero, qh)], axis=2)


def _head_pair_queries(q, n_q, slots):
    qh = _head_queries(q, n_q, slots)
    hr = qh.shape[2]
    qt = jnp.swapaxes(qh, 2, 3)
    cols = [jnp.pad(qt[:, h], ((0, 0), (0, 0), (h * hr, (N_HEADS - 1 - h) * hr))) for h in range(N_HEADS)]
    return jnp.stack([jnp.concatenate(cols[2 * pr:2 * pr + 2], axis=1) for pr in range(N_HEADS // 2)], axis=1)


def _head_major(t, n_q, slots):
    nb = t.shape[0] // n_q
    th = jnp.transpose(t.reshape(nb, n_q, N_HEADS, V_DIM), (0, 2, 1, 3))
    return jnp.pad(th, ((0, 0), (0, 0), (0, slots - n_q), (0, 0)))


def _pick_tile(n, pref):
    return pref if n % pref == 0 else n


def kernel(x_prompt, x_sample, mem_prompt, cache_k, cache_v, page_table, cache_mem_k, cache_mem_v, state_pool, rel_bias, norm_mix_g, w_in, lambda_q1, lambda_k1, lambda_q2, lambda_k2, subln_g, w_pool_grp, pool_scale, w_br_attn, w_br_pool, w_gate, w_out, norm_cross_g, w_cq, w_ck, w_cv, w_co, norm_ffn_g, w_router_grp, b_router_grp, w_router_exp, b_router_exp, w_exp_gate, w_exp_up, w_exp_down, norm_final_g):
    depth = w_in.shape[0]
    assert depth == 1, "single-layer step"
    l = 0
    b_p, s_p, d = x_prompt.shape
    b_s, s_s, _ = x_sample.shape
    n_p, n_s = b_p * s_p, b_s * s_s
    n_mem = mem_prompt.shape[1]
    page = cache_k.shape[2]
    past_len = page_table.shape[1] * page
    pool_w = state_pool.shape[-1]

    lam_init = 0.8 - 0.6 * math.exp(-0.3 * l)
    lam = (jnp.exp(jnp.sum(lambda_q1[l].astype(F32) * lambda_k1[l].astype(F32)))
           - jnp.exp(jnp.sum(lambda_q2[l].astype(F32) * lambda_k2[l].astype(F32))) + lam_init).reshape(1, 1)

    bf = lambda w: w.astype(BF16)
    w_in_bf, w_gate_bf, w_ba_bf, w_bp_bf, w_out_bf = bf(w_in[l]), bf(w_gate[l]), bf(w_br_attn[l]), bf(w_br_pool[l]), bf(w_out[l])
    w_cq_bf, w_ck_bf, w_cv_bf, w_co_bf = bf(w_cq[l]), bf(w_ck[l]), bf(w_cv[l]), bf(w_co[l])
    w_grp_bf = bf(w_pool_grp[l])
    w1_bf, w3_bf, w2_bf = bf(w_exp_gate[l]), bf(w_exp_up[l]), bf(w_exp_down[l])
    wr = jnp.concatenate([w_router_exp[l], w_router_grp[l]], axis=1).astype(F32)
    wr = jnp.pad(wr, ((0, 0), (0, LANES - wr.shape[1])))
    wr_hi = wr.astype(BF16)
    wr_lo = (wr - wr_hi.astype(F32)).astype(BF16)
    br = jnp.concatenate([b_router_exp[l], b_router_grp[l]]).astype(F32)
    br = jnp.pad(br, (0, LANES - br.shape[0])).reshape(1, LANES)

    def token_tail(x_tok, a, pooled, mem_k, mem_v, nb, tile, cross_tile):
        x1 = _merge(x_tok, a, pooled, norm_mix_g[l], w_gate_bf, w_ba_bf, w_bp_bf, w_out_bf, tile)
        x1 = x1.reshape(nb, -1, d)
        seq = x1.shape[1]
        x1 = jnp.pad(x1, ((0, 0), (0, -seq % BF16_ROWS), (0, 0)))
        x2 = _cross(x1, mem_k, mem_v, norm_cross_g[l], w_cq_bf, w_co_bf, max(cross_tile, BF16_ROWS))[:, :seq]
        n_tok = x_tok.shape[0]
        return _moe_final(x2.reshape(-1, d), norm_ffn_g[l], wr_hi, wr_lo, br, w1_bf, w3_bf, w2_bf,
                          norm_final_g, _pick_tile(n_tok, 1024), 2)

    tile_p = _pick_tile(n_p, 512)
    tq = _pick_tile(s_p, 256)
    q, k, v, u = _inproj(x_prompt.reshape(n_p, d), norm_mix_g[l], w_in_bf, tile_p)
    bias_tiles = _prompt_bias_tiles(rel_bias, tq, N_NEAR + 1)
    a = _prompt_attention(lam, q.reshape(b_p, s_p, d), k.reshape(b_p, s_p, d), v.reshape(b_p, s_p, d),
                          bias_tiles, subln_g[l], lam_init, tq)
    u3 = u.reshape(b_p, s_p, pool_w)
    pooled = _pool_mix(u3, w_grp_bf, pool_scale[l], _pick_tile(s_p, 256))
    mem_k, mem_v = _memkv(mem_prompt.reshape(b_p * n_mem, d), w_ck_bf, w_cv_bf, _pick_tile(b_p * n_mem, 512))
    mem_k, mem_v = mem_k.reshape(b_p, n_mem, d), mem_v.reshape(b_p, n_mem, d)
    y_prompt = token_tail(x_prompt.reshape(n_p, d), a.reshape(n_p, d), pooled.reshape(n_p, pool_w),
                          mem_k, mem_v, b_p, tile_p, _pick_tile(s_p, 512)).reshape(b_p, s_p, d)
    k_prompt = k.reshape(1, b_p, s_p, N_HEADS, V_DIM)
    v_prompt = v.reshape(1, b_p, s_p, N_HEADS, V_DIM)
    pool_prompt = u3[:, -POOL_CTX:][None]
    x_heads_dim = d // X_HEADS
    mem_k_prompt = mem_k.reshape(1, b_p, n_mem, X_HEADS, x_heads_dim)
    mem_v_prompt = mem_v.reshape(1, b_p, n_mem, X_HEADS, x_heads_dim)

    qs, ks, vs, us = _inproj(x_sample.reshape(n_s, d), norm_mix_g[l], w_in_bf, n_s)
    slots = -(-s_s // 8) * 8
    bias_past, bias_new = _decode_bias(rel_bias, past_len, slots, s_s, page)
    paged = lambda c: c.reshape(-1, page * N_HEADS, V_DIM)
    a_s = _decode_attention(page_table.astype(jnp.int32), lam, _head_pair_queries(qs, s_s, slots),
                            paged(cache_k), paged(cache_v), bias_past,
                            _head_major(ks, s_s, slots), _head_major(vs, s_s, slots), bias_new,
                            subln_g[l], lam_init, page)[:, :s_s]
    ctx = state_pool[l].astype(F32)
    full = jnp.concatenate([ctx, us.reshape(b_s, s_s, pool_w)], axis=1)
    seq_pad = -full.shape[1] % (2 * BF16_ROWS)
    pooled_s = _pool_mix(jnp.pad(full, ((0, 0), (0, seq_pad), (0, 0))), w_grp_bf, pool_scale[l],
                         full.shape[1] + seq_pad)[:, POOL_CTX:POOL_CTX + s_s]
    y_sample = token_tail(x_sample.reshape(n_s, d), a_s.reshape(n_s, d).astype(BF16), pooled_s.reshape(n_s, pool_w),
                          cache_mem_k[l].reshape(b_s, n_mem, d), cache_mem_v[l].reshape(b_s, n_mem, d),
                          b_s, n_s, s_s).reshape(b_s, s_s, d)
    k_sample = ks.reshape(1, b_s, s_s, N_HEADS, V_DIM)
    v_sample = vs.reshape(1, b_s, s_s, N_HEADS, V_DIM)
    pool_sample = full[:, -POOL_CTX:][None]

    return (y_prompt, y_sample, k_prompt, v_prompt, pool_prompt, mem_k_prompt, mem_v_prompt,
            k_sample, v_sample, pool_sample)
```

```python
import functools
import math

import jax
import jax.numpy as jnp
from jax import lax
from jax.experimental import pallas as pl
from jax.experimental.pallas import tpu as pltpu

F32 = jnp.float32
BF16 = jnp.bfloat16

RMS_EPS = 1e-6
N_HEADS = 8
HEAD_DIM = 64
V_DIM = 2 * HEAD_DIM
NUM_BUCKETS = 32
MAX_DISTANCE = 128
POOL_WINDOWS = (2, 4, 8, 16)
POOL_GROUP_DIM = 128
POOL_CTX = max(POOL_WINDOWS) - 1
X_HEADS = 4
N_GROUPS = 4
EXPERTS_PER_GROUP = 8
N_EXPERTS = N_GROUPS * EXPERTS_PER_GROUP
LANES = 128
BF16_ROWS = 16
VMEM_LIMIT = 56 * 1024 * 1024
PAGES_PER_STEP = 8
MOE_BLOCK = 288
NEG_INF = float("-inf")


def _cparams(*sem):
    return pltpu.CompilerParams(dimension_semantics=sem, vmem_limit_bytes=VMEM_LIMIT)


def _rms(x, g):
    return x * lax.rsqrt(jnp.mean(x * x, axis=-1, keepdims=True) + RMS_EPS) * g


def _dot(a, b):
    return jnp.dot(a, b, preferred_element_type=F32)


def _dot_nt(a, b):
    return lax.dot_general(a, b, (((1,), (1,)), ((), ())), preferred_element_type=F32)


def _const_spec(shape):
    zeros = (0,) * len(shape)
    return pl.BlockSpec(shape, lambda *_: zeros)


POOL_HALO = 16
POOL_SUB = 256


def _split_bf16(x):
    hi = x.astype(BF16)
    return hi, (x - hi.astype(F32)).astype(BF16)


def _pool_rows(u, prev, pos0, bc_ref, bp_ref, w_ref, scale_ref, o_ref):
    sub = bc_ref.shape[1]
    for r0 in range(0, u.shape[0], sub):
        us = u[r0:r0 + sub]
        u_hi, u_lo = _split_bf16(us)
        p_hi, p_lo = _split_bf16(prev if r0 == 0 else u[r0 - POOL_HALO:r0])
        pos = pos0 + r0 + lax.broadcasted_iota(jnp.int32, (sub, 1), 0)
        for g, w in enumerate(POOL_WINDOWS):
            sl = slice(g * POOL_GROUP_DIM, (g + 1) * POOL_GROUP_DIM)
            bc, bp = bc_ref[g], bp_ref[g]
            tot = (_dot(bc, u_hi[:, sl]) + _dot(bc, u_lo[:, sl])) + (_dot(bp, p_hi[:, sl]) + _dot(bp, p_lo[:, sl]))
            cnt = jnp.minimum(pos + 1, w).astype(F32)
            pooled = tot / cnt - us[:, sl]
            mixed = _dot(pooled.astype(BF16), w_ref[g]) * scale_ref[:, sl]
            o_ref[r0:r0 + sub, sl] = mixed.astype(o_ref.dtype)


def _pool_bands(tile):
    r = jnp.arange(tile, dtype=jnp.int32)[:, None]
    c = jnp.arange(tile, dtype=jnp.int32)[None, :]
    ch = jnp.arange(POOL_HALO, dtype=jnp.int32)[None, :] - POOL_HALO
    bc = jnp.stack([((r - c >= 0) & (r - c < w)) for w in POOL_WINDOWS]).astype(BF16)
    bp = jnp.stack([(r - ch < w) for w in POOL_WINDOWS]).astype(BF16)
    return bc, bp


def _inproj_kernel(x_ref, g_ref, w_ref, q_ref, k_ref, v_ref, u_ref, *, d):
    h = _rms(x_ref[...], g_ref[...]).astype(BF16)
    q_ref[...] = (_dot(h, w_ref[:, 0:d]) * (HEAD_DIM ** -0.5)).astype(BF16)
    k_ref[...] = _dot(h, w_ref[:, d:2 * d])
    v_ref[...] = _dot(h, w_ref[:, 2 * d:3 * d])
    u_ref[...] = _dot(h, w_ref[:, 3 * d:])


def _inproj(x, g, w_bf, tile):
    n, d = x.shape
    wu = w_bf.shape[1] - 3 * d
    row = lambda w: pl.BlockSpec((tile, w), lambda i: (i, 0))
    return pl.pallas_call(
        functools.partial(_inproj_kernel, d=d),
        grid=(n // tile,),
        in_specs=[row(d), _const_spec((1, d)), _const_spec(w_bf.shape)],
        out_specs=[row(d), row(d), row(d), row(wu)],
        out_shape=[jax.ShapeDtypeStruct((n, d), BF16), jax.ShapeDtypeStruct((n, d), F32),
                   jax.ShapeDtypeStruct((n, d), F32), jax.ShapeDtypeStruct((n, wu), F32)],
        compiler_params=_cparams("arbitrary"),
        name="inproj",
    )(x, g.reshape(1, d), w_bf)


def _head_norm(o, g, lam_init):
    return _rms(o, g) * (1.0 - lam_init)


def _lane_chunks(x):
    return [x[:, c * LANES:(c + 1) * LANES] for c in range(x.shape[1] // LANES)]


N_NEAR = 2


def _pattn_kernel(lam_ref, q_ref, k_ref, v_ref, bias_ref, g_ref, o_ref, qz_ref, kb_ref, vb_ref,
                  mxn_ref, mxf_ref, mbn_ref, mbf_ref, ls_ref, acc_ref, *s_refs, tq, nq, lam_init):
    s_len = q_ref.shape[0]
    near = N_NEAR * tq
    kb_ref[...] = k_ref[...].astype(BF16)
    vb_ref[...] = v_ref[...].astype(BF16)
    q = q_ref[...]
    lane = lax.broadcasted_iota(jnp.int32, q.shape, 1)
    zero = jnp.zeros_like(q)
    qz_ref[0] = jnp.where(lane < HEAD_DIM, q, zero)
    qz_ref[1] = jnp.where(lane >= HEAD_DIM, q, zero)
    c_far = bias_ref[N_NEAR][0:1, 0:1]
    mxn_ref[...] = jnp.full(mxn_ref.shape, NEG_INF, F32)
    mxf_ref[...] = jnp.full(mxf_ref.shape, NEG_INF, F32)

    def row_parts(kb):
        r0 = kb * tq
        parts = [(r0 + d * tq, r0 + (d + 1) * tq, True) for d in range(N_NEAR) if r0 + d * tq < s_len]
        if r0 + near < s_len:
            parts.append((r0 + near, s_len, False))
        return parts

    for kb in range(nq):
        r0 = kb * tq
        keys = kb_ref[r0:r0 + tq, :]
        for m in range(2):
            s = _dot_nt(qz_ref[m, r0:, :], keys)
            for lo, hi, is_near in row_parts(kb):
                t = s[lo - r0:hi - r0]
                if is_near:
                    t = t + bias_ref[(lo - r0) // tq]
                mx_ref = mxn_ref if is_near else mxf_ref
                mx_ref[m, lo:hi] = functools.reduce(jnp.maximum, _lane_chunks(t), mx_ref[m, lo:hi])
                s_refs[kb][m, lo - r0:hi - r0] = t

    mrow = jnp.maximum(jnp.max(mxn_ref[...], axis=2, keepdims=True),
                       jnp.max(mxf_ref[...], axis=2, keepdims=True) + c_far)
    mbn_ref[...] = jnp.broadcast_to(mrow, mbn_ref.shape)
    mbf_ref[...] = jnp.broadcast_to(mrow - c_far, mbf_ref.shape)
    ls_ref[...] = jnp.zeros(ls_ref.shape, F32)
    acc_ref[...] = jnp.zeros(acc_ref.shape, F32)

    for kb in range(nq):
        r0 = kb * tq
        vals = vb_ref[r0:r0 + tq, :]
        for m in range(2):
            for lo, hi, is_near in row_parts(kb):
                mb = (mbn_ref if is_near else mbf_ref)[m, lo:hi]
                p = [jnp.exp(c - mb) for c in _lane_chunks(s_refs[kb][m, lo - r0:hi - r0])]
                ls_ref[m, lo:hi] += functools.reduce(jnp.add, p)
                acc_ref[m, lo:hi] += _dot(jnp.concatenate(p, axis=1).astype(BF16), vals)

    o = acc_ref[...] / jnp.sum(ls_ref[...], axis=2, keepdims=True)
    o = o[0] - lam_ref[0, 0] * o[1]
    o_ref[...] = _head_norm(o, g_ref[...], lam_init).astype(o_ref.dtype)


def _prompt_attention(lam, q, k, v, bias_tiles, subln_g, lam_init, tq):
    b, s, d = q.shape
    nq = s // tq
    assert tq + 1 >= MAX_DISTANCE and bias_tiles.shape[1] == N_NEAR + 1
    seq = pl.BlockSpec((None, s, V_DIM), lambda h, bi: (bi, 0, h))
    stat = pltpu.VMEM((2, s, LANES), F32)
    return pl.pallas_call(
        functools.partial(_pattn_kernel, tq=tq, nq=nq, lam_init=lam_init),
        grid=(N_HEADS, b),
        in_specs=[pl.BlockSpec(memory_space=pltpu.SMEM), seq, seq, seq,
                  pl.BlockSpec((None, N_NEAR + 1, tq, tq), lambda h, bi: (h, 0, 0, 0)),
                  _const_spec((1, V_DIM))],
        out_specs=seq,
        out_shape=jax.ShapeDtypeStruct((b, s, d), BF16),
        scratch_shapes=[pltpu.VMEM((2, s, V_DIM), BF16), pltpu.VMEM((s, V_DIM), BF16), pltpu.VMEM((s, V_DIM), BF16),
                        stat, stat, stat, stat, stat, pltpu.VMEM((2, s, V_DIM), F32)]
                       + [pltpu.VMEM((2, s - kb * tq, tq), F32) for kb in range(nq)],
        compiler_params=_cparams("arbitrary", "arbitrary"),
        name="prompt_attn",
    )(lam, q, k, v, bias_tiles, subln_g.reshape(1, V_DIM))


def _dattn_kernel(pt_ref, lam_ref, q_ref, *refs, page, lam_init):
    del pt_ref
    npg = PAGES_PER_STEP
    k_refs, v_refs = refs[:npg], refs[npg:2 * npg]
    bias_ref, kn_ref, vn_ref, biasn_ref, g_ref, o_ref, m_ref, l_ref, acc_ref = refs[2 * npg:]
    g_idx, b = pl.program_id(0), pl.program_id(1)
    hr = m_ref.shape[1] // N_HEADS
    half = hr // 2

    @pl.when(g_idx == 0)
    def _():
        m_ref[b] = jnp.full(m_ref.shape[1:], NEG_INF, F32)
        l_ref[b] = jnp.zeros(l_ref.shape[1:], F32)
        acc_ref[b] = jnp.zeros(acc_ref.shape[1:], F32)

    def head_rows(ref, h):
        return ref[pl.ds(h, page, stride=N_HEADS), :].astype(BF16)

    def update(s, v_blocks):
        m_old = m_ref[b]
        m_new = jnp.maximum(m_old, jnp.max(s, axis=1, keepdims=True))
        alpha = jnp.exp(m_old - m_new)
        p = jnp.exp(s - m_new)
        l_ref[b] = alpha * l_ref[b] + jnp.sum(p, axis=1, keepdims=True)
        m_ref[b] = m_new
        pb = p.astype(BF16)
        for h in range(N_HEADS):
            rs = slice(h * hr, (h + 1) * hr)
            pv = None
            for i, v in enumerate(v_blocks[h]):
                t = _dot(pb[rs, i * page:(i + 1) * page], v)
                pv = t if pv is None else pv + t
            acc_ref[b, h] = alpha[rs] * acc_ref[b, h] + pv

    def page_scores(k_of_head):
        acc = None
        for pr in range(N_HEADS // 2):
            keys = jnp.concatenate([k_of_head(2 * pr), k_of_head(2 * pr + 1)], axis=1)
            t = _dot(keys, q_ref[pr])
            acc = t if acc is None else acc + t
        return acc.T

    s = jnp.concatenate([page_scores(lambda h, kr=kr: head_rows(kr, h)) for kr in k_refs], axis=1)
    update(s + bias_ref[...], [[head_rows(vr, h) for vr in v_refs] for h in range(N_HEADS)])

    @pl.when(g_idx == pl.num_programs(0) - 1)
    def _():
        def pad_page(x):
            return jnp.concatenate([x, jnp.zeros((page - x.shape[0], x.shape[1]), x.dtype)], axis=0).astype(BF16)

        sn = page_scores(lambda h: pad_page(kn_ref[h]))
        update(sn + biasn_ref[...], [[pad_page(vn_ref[h])] for h in range(N_HEADS)])
        linv = 1.0 / l_ref[b]
        g = g_ref[...]
        for h in range(N_HEADS):
            o = acc_ref[b, h] * linv[h * hr:(h + 1) * hr]
            o = o[:half] - lam_ref[0, 0] * o[half:]
            o_ref[b, :, h * V_DIM:(h + 1) * V_DIM] = _head_norm(o, g, lam_init)


def _decode_attention(page_table, lam, q_heads, cache_k, cache_v, bias_past, k_new, v_new, bias_new,
                      subln_g, lam_init, page):
    nb, n_pages = page_table.shape
    npg = PAGES_PER_STEP
    rows = q_heads.shape[-1]
    hr = rows // N_HEADS
    slots = hr // 2
    d = N_HEADS * V_DIM
    assert rows == LANES

    def page_spec(i):
        return pl.BlockSpec((None, page * N_HEADS, V_DIM), lambda g, b, pt: (pt[b, g * npg + i], 0, 0))

    seq = lambda *blk: pl.BlockSpec((None,) + blk, lambda g, b, pt: (b,) + (0,) * len(blk))
    in_specs = ([pl.BlockSpec(memory_space=pltpu.SMEM), seq(N_HEADS // 2, 2 * V_DIM, rows)]
                + [page_spec(i) for i in range(npg)] * 2
                + [pl.BlockSpec((rows, npg * page), lambda g, b, pt: (0, g)),
                   seq(N_HEADS, slots, V_DIM), seq(N_HEADS, slots, V_DIM),
                   pl.BlockSpec((rows, page), lambda g, b, pt: (0, 0)),
                   pl.BlockSpec((1, V_DIM), lambda g, b, pt: (0, 0))])
    return pl.pallas_call(
        functools.partial(_dattn_kernel, page=page, lam_init=lam_init),
        grid_spec=pltpu.PrefetchScalarGridSpec(
            num_scalar_prefetch=1,
            grid=(n_pages // npg, nb),
            in_specs=in_specs,
            out_specs=pl.BlockSpec((nb, slots, d), lambda g, b, pt: (0, 0, 0)),
            scratch_shapes=[pltpu.VMEM((nb, rows, 1), F32), pltpu.VMEM((nb, rows, 1), F32),
                            pltpu.VMEM((nb, N_HEADS, hr, V_DIM), F32)]),
        out_shape=jax.ShapeDtypeStruct((nb, slots, d), F32),
        compiler_params=_cparams("arbitrary", "arbitrary"),
        name="decode_attn",
    )(page_table, lam, q_heads, *([cache_k] * npg), *([cache_v] * npg), bias_past, k_new, v_new, bias_new,
      subln_g.reshape(1, V_DIM))


def _pool_kernel(u_ref, prev_ref, bc_ref, bp_ref, w_ref, scale_ref, o_ref, *, tile):
    i = pl.program_id(1)
    prev = jnp.where(i > 0, prev_ref[...], 0.0)
    _pool_rows(u_ref[...], prev, i * tile, bc_ref, bp_ref, w_ref, scale_ref, o_ref)


def _pool_mix(u, w_grp_bf, scale, tile):
    b, s, wdt = u.shape
    bc, bp = _pool_bands(min(tile, POOL_SUB))
    assert tile % bc.shape[1] == 0
    per_halo = tile // POOL_HALO
    return pl.pallas_call(
        functools.partial(_pool_kernel, tile=tile),
        grid=(b, s // tile),
        in_specs=[pl.BlockSpec((None, tile, wdt), lambda bi, i: (bi, i, 0)),
                  pl.BlockSpec((None, POOL_HALO, wdt), lambda bi, i: (bi, jnp.maximum(i * per_halo - 1, 0), 0)),
                  _const_spec(bc.shape), _const_spec(bp.shape), _const_spec(w_grp_bf.shape),
                  _const_spec((1, wdt))],
        out_specs=pl.BlockSpec((None, tile, wdt), lambda bi, i: (bi, i, 0)),
        out_shape=jax.ShapeDtypeStruct((b, s, wdt), BF16),
        compiler_params=_cparams("arbitrary", "arbitrary"),
        name="pool_mix",
    )(u, u, bc, bp, w_grp_bf, scale.reshape(1, wdt))


def _merge_kernel(x_ref, a_ref, p_ref, g_ref, wg_ref, wa_ref, wp_ref, wo_ref, o_ref, *, d):
    x = x_ref[...]
    h = _rms(x, g_ref[...]).astype(BF16)
    ga = jax.nn.sigmoid(_dot(h, wg_ref[:, :d]))
    merged = ga * _dot(a_ref[...], wa_ref[...])
    gp = jax.nn.sigmoid(_dot(h, wg_ref[:, d:]))
    merged += gp * _dot(p_ref[...], wp_ref[...])
    o_ref[...] = x + _dot(merged.astype(BF16), wo_ref[...])


def _merge(x, a, pooled, g, w_gate, w_br_attn, w_br_pool, w_out, tile):
    n, d = x.shape
    row = lambda w: pl.BlockSpec((tile, w), lambda i: (i, 0))
    return pl.pallas_call(
        functools.partial(_merge_kernel, d=d),
        grid=(n // tile,),
        in_specs=[row(d), row(a.shape[1]), row(pooled.shape[1]), _const_spec((1, d)),
                  _const_spec(w_gate.shape), _const_spec(w_br_attn.shape), _const_spec(w_br_pool.shape),
                  _const_spec(w_out.shape)],
        out_specs=row(d),
        out_shape=jax.ShapeDtypeStruct((n, d), F32),
        compiler_params=_cparams("arbitrary"),
        name="merge",
    )(x, a, pooled, g.reshape(1, d), w_gate, w_br_attn, w_br_pool, w_out)


def _memkv_kernel(x_ref, wk_ref, wv_ref, k_ref, v_ref):
    x = x_ref[...].astype(BF16)
    k_ref[...] = _dot(x, wk_ref[...])
    v_ref[...] = _dot(x, wv_ref[...])


def _memkv(x, wk, wv, tile):
    n, d = x.shape
    row = pl.BlockSpec((tile, d), lambda i: (i, 0))
    return pl.pallas_call(
        _memkv_kernel,
        grid=(n // tile,),
        in_specs=[row, _const_spec(wk.shape), _const_spec(wv.shape)],
        out_specs=[row, row],
        out_shape=[jax.ShapeDtypeStruct((n, d), F32)] * 2,
        compiler_params=_cparams("arbitrary"),
        name="mem_kv",
    )(x, wk, wv)


def _cross_kernel(x_ref, mk_ref, mv_ref, g_ref, wq_ref, wo_ref, o_ref, *, dh):
    x = x_ref[...]
    h = _rms(x, g_ref[...]).astype(BF16)
    q = (_dot(h, wq_ref[...]) * (dh ** -0.5)).astype(BF16)
    out = x
    for hh in range(X_HEADS):
        sl = slice(hh * dh, (hh + 1) * dh)
        s = _dot_nt(q[:, sl], mk_ref[:, sl].astype(BF16))
        p = jnp.exp(s - jnp.max(s, axis=1, keepdims=True))
        o = _dot(p.astype(BF16), mv_ref[:, sl].astype(BF16)) / jnp.sum(p, axis=1, keepdims=True)
        out += _dot(o.astype(BF16), wo_ref[sl, :])
    o_ref[...] = out


def _cross(x, mem_k, mem_v, g, w_cq, w_co, tile):
    b, s, d = x.shape
    m = mem_k.shape[1]
    xs = pl.BlockSpec((None, tile, d), lambda bi, i: (bi, i, 0))
    ms = pl.BlockSpec((None, m, d), lambda bi, i: (bi, 0, 0))
    return pl.pallas_call(
        functools.partial(_cross_kernel, dh=d // X_HEADS),
        grid=(b, s // tile),
        in_specs=[xs, ms, ms, _const_spec((1, d)), _const_spec(w_cq.shape), _const_spec(w_co.shape)],
        out_specs=xs,
        out_shape=jax.ShapeDtypeStruct((b, s, d), F32),
        compiler_params=_cparams("arbitrary", "arbitrary"),
        name="cross_attn",
    )(x, mem_k, mem_v, g.reshape(1, d), w_cq, w_co)


def _router_gates(h, wr_hi, wr_lo, br):
    h_hi, h_lo = _split_bf16(h)
    logits = _dot(h_hi, wr_hi) + (_dot(h_lo, wr_hi) + _dot(h_hi, wr_lo)) + br
    lane = lax.broadcasted_iota(jnp.int32, logits.shape, 1)
    big = jnp.int32(LANES)
    first = lambda mask: jnp.min(jnp.where(mask, lane, big), axis=1, keepdims=True)
    rmax = lambda x: jnp.max(x, axis=1, keepdims=True)

    gmask = (lane >= N_EXPERTS) & (lane < N_EXPERTS + N_GROUPS)
    gl = jnp.where(gmask, logits, NEG_INF)
    gmax = rmax(gl)
    grp = first(gl == gmax) - N_EXPERTS
    p_grp = 1.0 / jnp.sum(jnp.exp(gl - gmax), axis=1, keepdims=True)

    lo = grp * EXPERTS_PER_GROUP
    el = jnp.where((lane >= lo) & (lane < lo + EXPERTS_PER_GROUP), logits, NEG_INF)
    v1 = rmax(el)
    i1 = first(el == v1)
    el2 = jnp.where(lane == i1, NEG_INF, el)
    v2 = rmax(el2)
    i2 = first(el2 == v2)
    t = jnp.exp(v2 - v1)
    w1 = p_grp / (1.0 + t)
    return jnp.where(lane == i1, w1, 0.0) + jnp.where(lane == i2, w1 * t, 0.0), grp


def _split3_bf16(x):
    a = x.astype(BF16)
    r = x - a.astype(F32)
    b = r.astype(BF16)
    return a, b, (r - b.astype(F32)).astype(BF16)


def _moe_kernel(x_ref, g_ref, wrh_ref, wrl_ref, br_ref, w1_ref, w3_ref, w2_ref, gf_ref, o_ref,
                hs_ref, gs_ref, ys_ref, pt_ref, meta_ref, *, tile, tp, n_sub):
    sg = pl.program_id(1)
    per_step = EXPERTS_PER_GROUP // n_sub

    @pl.when(sg == 0)
    def _route():
        h = _rms(x_ref[...], g_ref[...])
        gates, grp = _router_gates(h, wrh_ref[...], wrl_ref[...], br_ref[...])
        lane = lax.broadcasted_iota(jnp.int32, gates.shape, 1)
        member = lane == grp
        r = lax.broadcasted_iota(jnp.int32, (tile, tile), 0)
        c = lax.broadcasted_iota(jnp.int32, (tile, tile), 1)
        tri = jnp.where(r >= c, 1.0, 0.0).astype(BF16)
        csum = _dot(tri, jnp.where(member, 1.0, 0.0).astype(BF16))
        cnt = csum[tile - 1:tile, :]
        padded = jnp.ceil(cnt * (1.0 / BF16_ROWS)) * BF16_ROWS
        lane_row = lane[0:1, :]
        off = jnp.zeros_like(cnt)
        run = jnp.zeros((1, 1), F32)
        for gg in range(N_GROUPS):
            off = jnp.where(lane_row == gg, run, off)
            run = run + padded[:, gg:gg + 1]
        pos = jnp.sum(jnp.where(member, off + csum - 1.0, 0.0), axis=1, keepdims=True)
        pos_row = jnp.transpose(jnp.broadcast_to(pos, (tile, LANES)))[0:1, :]
        pt = jnp.where(lax.broadcasted_iota(jnp.int32, (tile, tp), 1) == pos.astype(jnp.int32), 1.0, 0.0)
        p = jnp.where(lax.broadcasted_iota(jnp.int32, (tp, tile), 0) == pos_row.astype(jnp.int32), 1.0, 0.0)
        p = p.astype(BF16)
        pt_ref[...] = pt.astype(BF16)
        hs_ref[0:tp, :] = _dot(p, h.astype(BF16)).astype(BF16)
        hs_ref[tp:, :] = jnp.zeros((hs_ref.shape[0] - tp, hs_ref.shape[1]), BF16)
        grel = gates
        for gg in range(1, N_GROUPS):
            grel = jnp.where(grp == gg, pltpu.roll(gates, LANES - gg * EXPERTS_PER_GROUP, axis=1), grel)
        gs_ref[0:tp, :] = functools.reduce(jnp.add, [_dot(p, t) for t in _split3_bf16(grel)])
        gs_ref[tp:, :] = jnp.zeros((gs_ref.shape[0] - tp, gs_ref.shape[1]), F32)
        ys_ref[...] = jnp.zeros(ys_ref.shape, F32)
        for gg in range(N_GROUPS):
            pick = lambda v: jnp.sum(jnp.where(lane_row == gg, v, 0.0)).astype(jnp.int32)
            meta_ref[0, gg] = pick(off)
            meta_ref[1, gg] = pick(cnt)

    grp_id = sg // n_sub
    e0 = (sg % n_sub) * per_step
    start = meta_ref[0, grp_id]
    cnt = meta_ref[1, grp_id]
    lane8 = lax.broadcasted_iota(jnp.int32, (MOE_BLOCK, LANES), 1)

    def block(i, carry):
        r0 = pl.multiple_of(start + i * MOE_BLOCK, BF16_ROWS)
        rows = hs_ref[pl.ds(r0, MOE_BLOCK), :]
        ridx = r0 + lax.broadcasted_iota(jnp.int32, (MOE_BLOCK, 1), 0)
        gts = jnp.where(ridx < start + cnt, gs_ref[pl.ds(r0, MOE_BLOCK), :], 0.0)
        y = None
        for j in range(per_step):
            gate = jnp.sum(jnp.where(lane8 == e0 + j, gts, 0.0), axis=1, keepdims=True)
            h1 = _dot(rows, w1_ref[j])
            h3 = _dot(rows, w3_ref[j])
            act = (h1 * jax.nn.sigmoid(h1)) * h3 * gate
            t = _dot(act.astype(BF16), w2_ref[j])
            y = t if y is None else y + t
        ys_ref[pl.ds(r0, MOE_BLOCK), :] += y
        return carry

    lax.fori_loop(0, (cnt + MOE_BLOCK - 1) // MOE_BLOCK, block, 0)

    @pl.when(sg == pl.num_programs(1) - 1)
    def _finish():
        pt = pt_ref[...]
        y_hi, y_lo = _split_bf16(ys_ref[0:tp, :])
        o_ref[...] = _rms(x_ref[...] + (_dot(pt, y_hi) + _dot(pt, y_lo)), gf_ref[...])


def _moe_final(x, g, wr_hi, wr_lo, br, w1, w3, w2, g_final, tile, n_sub):
    n, d = x.shape
    n_e, _, f = w1.shape
    per_step = EXPERTS_PER_GROUP // n_sub
    tp = tile + LANES
    rows = tp + MOE_BLOCK
    row = pl.BlockSpec((tile, d), lambda i, s: (i, 0))
    cst = lambda shape: pl.BlockSpec(shape, lambda i, s: (0,) * len(shape))
    return pl.pallas_call(
        functools.partial(_moe_kernel, tile=tile, tp=tp, n_sub=n_sub),
        grid=(n // tile, N_GROUPS * n_sub),
        in_specs=[row, cst((1, d)), cst(wr_hi.shape), cst(wr_lo.shape), cst((1, LANES)),
                  pl.BlockSpec((per_step, d, f), lambda i, s: (s, 0, 0)),
                  pl.BlockSpec((per_step, d, f), lambda i, s: (s, 0, 0)),
                  pl.BlockSpec((per_step, f, d), lambda i, s: (s, 0, 0)),
                  cst((1, d))],
        out_specs=row,
        out_shape=jax.ShapeDtypeStruct((n, d), F32),
        scratch_shapes=[pltpu.VMEM((rows, d), BF16), pltpu.VMEM((rows, LANES), F32), pltpu.VMEM((rows, d), F32),
                        pltpu.VMEM((tile, tp), BF16), pltpu.SMEM((2, N_GROUPS), jnp.int32)],
        compiler_params=_cparams("arbitrary", "arbitrary"),
        name="moe_final",
    )(x, g.reshape(1, d), wr_hi, wr_lo, br, w1, w3, w2, g_final.reshape(1, d))


def _t5_bucket(rel):
    n = jnp.maximum(-rel, 0)
    max_exact = NUM_BUCKETS // 2
    nf = jnp.maximum(n, max_exact).astype(F32)
    large = max_exact + (jnp.log(nf / max_exact) / math.log(MAX_DISTANCE / max_exact)
                         * (NUM_BUCKETS - max_exact)).astype(jnp.int32)
    large = jnp.minimum(large, NUM_BUCKETS - 1)
    return jnp.where(n < max_exact, n, large)


def _distance_bias(rel_bias, n_dist, descending=False):
    dist = jnp.arange(n_dist, dtype=jnp.int32)
    bucket = _t5_bucket(-(n_dist - 1 - dist if descending else dist))
    onehot = (bucket[:, None] == jnp.arange(NUM_BUCKETS, dtype=jnp.int32)[None, :]).astype(F32)
    return jnp.dot(onehot, rel_bias.astype(F32), precision=lax.Precision.HIGHEST).T


def _prompt_bias_tiles(rel_bias, tq, nd):
    p = 2 * tq
    table = _distance_bias(rel_bias, nd * tq)
    ext = jnp.pad(table, ((0, 0), (tq, 1)), constant_values=NEG_INF)
    c = jnp.stack([jnp.concatenate([ext[:, d * tq + 1:d * tq + tq + 1][:, ::-1],
                                    ext[:, d * tq + tq + 1:d * tq + p + 1][:, ::-1]], axis=1)
                   for d in range(nd)], axis=1)
    m = jnp.tile(c, (1, 1, tq))[:, :, :tq * (p - 1)].reshape(N_HEADS, nd, tq, p - 1)
    return m[..., :tq]


def _decode_bias(rel_bias, past_len, slots, n_new, page):
    n_dist = past_len + slots
    table = _distance_bias(rel_bias, n_dist, descending=True)
    past = jnp.stack([table[:, slots - 1 - j:slots - 1 - j + past_len] for j in range(slots)], axis=1)
    jq = jnp.arange(slots, dtype=jnp.int32)[:, None]
    jk = jnp.arange(page, dtype=jnp.int32)[None, :]
    visible = (jk <= jq) & (jk < n_new)
    near = table[:, n_dist - slots:][:, ::-1]
    new = jnp.where(visible[None], near[:, jnp.clip(jq - jk, 0, slots - 1)], NEG_INF)
    both = lambda t: jnp.broadcast_to(t[:, None], (N_HEADS, 2) + t.shape[1:]).reshape(N_HEADS * 2 * slots, -1)
    return both(past), both(new)


def _head_queries(q, n_q, slots):
    nb = q.shape[0] // n_q
    qh = jnp.transpose(q.reshape(nb, n_q, N_HEADS, V_DIM), (0, 2, 1, 3))
    qh = jnp.pad(qh, ((0, 0), (0, 0), (0, slots - n_q), (0, 0)))
    first = jnp.arange(V_DIM) < HEAD_DIM
    zero = jnp.zeros((), q.dtype)
    return jnp.concatenate([jnp.where(first, qh, zero), jnp.where(first, zero, qh)], axis=2)


def _head_pair_queries(q, n_q, slots):
    qh = _head_queries(q, n_q, slots)
    hr = qh.shape[2]
    qt = jnp.swapaxes(qh, 2, 3)
    cols = [jnp.pad(qt[:, h], ((0, 0), (0, 0), (h * hr, (N_HEADS - 1 - h) * hr))) for h in range(N_HEADS)]
    return jnp.stack([jnp.concatenate(cols[2 * pr:2 * pr + 2], axis=1) for pr in range(N_HEADS // 2)], axis=1)


def _head_major(t, n_q, slots):
    nb = t.shape[0] // n_q
    th = jnp.transpose(t.reshape(nb, n_q, N_HEADS, V_DIM), (0, 2, 1, 3))
    return jnp.pad(th, ((0, 0), (0, 0), (0, slots - n_q), (0, 0)))


def _pick_tile(n, pref):
    return pref if n % pref == 0 else n


def kernel(x_prompt, x_sample, mem_prompt, cache_k, cache_v, page_table, cache_mem_k, cache_mem_v, state_pool, rel_bias, norm_mix_g, w_in, lambda_q1, lambda_k1, lambda_q2, lambda_k2, subln_g, w_pool_grp, pool_scale, w_br_attn, w_br_pool, w_gate, w_out, norm_cross_g, w_cq, w_ck, w_cv, w_co, norm_ffn_g, w_router_grp, b_router_grp, w_router_exp, b_router_exp, w_exp_gate, w_exp_up, w_exp_down, norm_final_g):
    depth = w_in.shape[0]
    assert depth == 1, "single-layer step"
    l = 0
    b_p, s_p, d = x_prompt.shape
    b_s, s_s, _ = x_sample.shape
    n_p, n_s = b_p * s_p, b_s * s_s
    n_mem = mem_prompt.shape[1]
    page = cache_k.shape[2]
    past_len = page_table.shape[1] * page
    pool_w = state_pool.shape[-1]

    lam_init = 0.8 - 0.6 * math.exp(-0.3 * l)
    lam = (jnp.exp(jnp.sum(lambda_q1[l].astype(F32) * lambda_k1[l].astype(F32)))
           - jnp.exp(jnp.sum(lambda_q2[l].astype(F32) * lambda_k2[l].astype(F32))) + lam_init).reshape(1, 1)

    bf = lambda w: w.astype(BF16)
    w_in_bf, w_gate_bf, w_ba_bf, w_bp_bf, w_out_bf = bf(w_in[l]), bf(w_gate[l]), bf(w_br_attn[l]), bf(w_br_pool[l]), bf(w_out[l])
    w_cq_bf, w_ck_bf, w_cv_bf, w_co_bf = bf(w_cq[l]), bf(w_ck[l]), bf(w_cv[l]), bf(w_co[l])
    w_grp_bf = bf(w_pool_grp[l])
    w1_bf, w3_bf, w2_bf = bf(w_exp_gate[l]), bf(w_exp_up[l]), bf(w_exp_down[l])
    wr = jnp.concatenate([w_router_exp[l], w_router_grp[l]], axis=1).astype(F32)
    wr = jnp.pad(wr, ((0, 0), (0, LANES - wr.shape[1])))
    wr_hi = wr.astype(BF16)
    wr_lo = (wr - wr_hi.astype(F32)).astype(BF16)
    br = jnp.concatenate([b_router_exp[l], b_router_grp[l]]).astype(F32)
    br = jnp.pad(br, (0, LANES - br.shape[0])).reshape(1, LANES)

    def token_tail(x_tok, a, pooled, mem_k, mem_v, nb, tile, cross_tile):
        x1 = _merge(x_tok, a, pooled, norm_mix_g[l], w_gate_bf, w_ba_bf, w_bp_bf, w_out_bf, tile)
        x1 = x1.reshape(nb, -1, d)
        seq = x1.shape[1]
        x1 = jnp.pad(x1, ((0, 0), (0, -seq % BF16_ROWS), (0, 0)))
        x2 = _cross(x1, mem_k, mem_v, norm_cross_g[l], w_cq_bf, w_co_bf, max(cross_tile, BF16_ROWS))[:, :seq]
        n_tok = x_tok.shape[0]
        return _moe_final(x2.reshape(-1, d), norm_ffn_g[l], wr_hi, wr_lo, br, w1_bf, w3_bf, w2_bf,
                          norm_final_g, _pick_tile(n_tok, 1024), 2)

    tile_p = _pick_tile(n_p, 512)
    tq = _pick_tile(s_p, 256)
    q, k, v, u = _inproj(x_prompt.reshape(n_p, d), norm_mix_g[l], w_in_bf, tile_p)
    bias_tiles = _prompt_bias_tiles(rel_bias, tq, N_NEAR + 1)
    a = _prompt_attention(lam, q.reshape(b_p, s_p, d), k.reshape(b_p, s_p, d), v.reshape(b_p, s_p, d),
                          bias_tiles, subln_g[l], lam_init, tq)
    u3 = u.reshape(b_p, s_p, pool_w)
    pooled = _pool_mix(u3, w_grp_bf, pool_scale[l], _pick_tile(s_p, 256))
    mem_k, mem_v = _memkv(mem_prompt.reshape(b_p * n_mem, d), w_ck_bf, w_cv_bf, _pick_tile(b_p * n_mem, 512))
    mem_k, mem_v = mem_k.reshape(b_p, n_mem, d), mem_v.reshape(b_p, n_mem, d)
    y_prompt = token_tail(x_prompt.reshape(n_p, d), a.reshape(n_p, d), pooled.reshape(n_p, pool_w),
                          mem_k, mem_v, b_p, tile_p, _pick_tile(s_p, 512)).reshape(b_p, s_p, d)
    k_prompt = k.reshape(1, b_p, s_p, N_HEADS, V_DIM)
    v_prompt = v.reshape(1, b_p, s_p, N_HEADS, V_DIM)
    pool_prompt = u3[:, -POOL_CTX:][None]
    x_heads_dim = d // X_HEADS
    mem_k_prompt = mem_k.reshape(1, b_p, n_mem, X_HEADS, x_heads_dim)
    mem_v_prompt = mem_v.reshape(1, b_p, n_mem, X_HEADS, x_heads_dim)

    qs, ks, vs, us = _inproj(x_sample.reshape(n_s, d), norm_mix_g[l], w_in_bf, n_s)
    slots = -(-s_s // 8) * 8
    bias_past, bias_new = _decode_bias(rel_bias, past_len, slots, s_s, page)
    paged = lambda c: c.reshape(-1, page * N_HEADS, V_DIM)
    a_s = _decode_attention(page_table.astype(jnp.int32), lam, _head_pair_queries(qs, s_s, slots),
                            paged(cache_k), paged(cache_v), bias_past,
                            _head_major(ks, s_s, slots), _head_major(vs, s_s, slots), bias_new,
                            subln_g[l], lam_init, page)[:, :s_s]
    ctx = state_pool[l].astype(F32)
    full = jnp.concatenate([ctx, us.reshape(b_s, s_s, pool_w)], axis=1)
    slot = full.shape[1] + (-full.shape[1] % (2 * BF16_ROWS))
    stream = jnp.pad(full, ((0, 0), (0, slot - full.shape[1]), (0, 0))).reshape(1, b_s * slot, pool_w)
    pooled_s = _pool_mix(stream, w_grp_bf, pool_scale[l], _pick_tile(b_s * slot, 256))
    pooled_s = pooled_s.reshape(b_s, slot, pool_w)[:, POOL_CTX:POOL_CTX + s_s]
    y_sample = token_tail(x_sample.reshape(n_s, d), a_s.reshape(n_s, d).astype(BF16), pooled_s.reshape(n_s, pool_w),
                          cache_mem_k[l].reshape(b_s, n_mem, d), cache_mem_v[l].reshape(b_s, n_mem, d),
                          b_s, n_s, s_s).reshape(b_s, s_s, d)
    k_sample = ks.reshape(1, b_s, s_s, N_HEADS, V_DIM)
    v_sample = vs.reshape(1, b_s, s_s, N_HEADS, V_DIM)
    pool_sample = full[:, -POOL_CTX:][None]

    return (y_prompt, y_sample, k_prompt, v_prompt, pool_prompt, mem_k_prompt, mem_v_prompt,
            k_sample, v_sample, pool_sample)
```

```python
import functools
import math

import jax
import jax.numpy as jnp
from jax import lax
from jax.experimental import pallas as pl
from jax.experimental.pallas import tpu as pltpu

F32 = jnp.float32
BF16 = jnp.bfloat16

RMS_EPS = 1e-6
N_HEADS = 8
HEAD_DIM = 64
V_DIM = 2 * HEAD_DIM
NUM_BUCKETS = 32
MAX_DISTANCE = 128
POOL_WINDOWS = (2, 4, 8, 16)
POOL_GROUP_DIM = 128
POOL_CTX = max(POOL_WINDOWS) - 1
X_HEADS = 4
N_GROUPS = 4
EXPERTS_PER_GROUP = 8
N_EXPERTS = N_GROUPS * EXPERTS_PER_GROUP
LANES = 128
BF16_ROWS = 16
VMEM_LIMIT = 56 * 1024 * 1024
PAGES_PER_STEP = 16
MOE_BLOCK = 288
NEG_INF = float("-inf")


def _cparams(*sem):
    return pltpu.CompilerParams(dimension_semantics=sem, vmem_limit_bytes=VMEM_LIMIT)


def _rms(x, g):
    return x * lax.rsqrt(jnp.mean(x * x, axis=-1, keepdims=True) + RMS_EPS) * g


def _dot(a, b):
    return jnp.dot(a, b, preferred_element_type=F32)


def _dot_nt(a, b):
    return lax.dot_general(a, b, (((1,), (1,)), ((), ())), preferred_element_type=F32)


def _const_spec(shape):
    zeros = (0,) * len(shape)
    return pl.BlockSpec(shape, lambda *_: zeros)


POOL_HALO = 16
POOL_SUB = 256


def _split_bf16(x):
    hi = x.astype(BF16)
    return hi, (x - hi.astype(F32)).astype(BF16)


def _pool_rows(u, prev, pos0, bc_ref, bp_ref, w_ref, scale_ref, o_ref):
    sub = bc_ref.shape[1]
    for r0 in range(0, u.shape[0], sub):
        us = u[r0:r0 + sub]
        u_hi, u_lo = _split_bf16(us)
        p_hi, p_lo = _split_bf16(prev if r0 == 0 else u[r0 - POOL_HALO:r0])
        pos = pos0 + r0 + lax.broadcasted_iota(jnp.int32, (sub, 1), 0)
        for g, w in enumerate(POOL_WINDOWS):
            sl = slice(g * POOL_GROUP_DIM, (g + 1) * POOL_GROUP_DIM)
            bc, bp = bc_ref[g], bp_ref[g]
            tot = (_dot(bc, u_hi[:, sl]) + _dot(bc, u_lo[:, sl])) + (_dot(bp, p_hi[:, sl]) + _dot(bp, p_lo[:, sl]))
            cnt = jnp.minimum(pos + 1, w).astype(F32)
            pooled = tot / cnt - us[:, sl]
            mixed = _dot(pooled.astype(BF16), w_ref[g]) * scale_ref[:, sl]
            o_ref[r0:r0 + sub, sl] = mixed.astype(o_ref.dtype)


def _pool_bands(tile):
    r = jnp.arange(tile, dtype=jnp.int32)[:, None]
    c = jnp.arange(tile, dtype=jnp.int32)[None, :]
    ch = jnp.arange(POOL_HALO, dtype=jnp.int32)[None, :] - POOL_HALO
    bc = jnp.stack([((r - c >= 0) & (r - c < w)) for w in POOL_WINDOWS]).astype(BF16)
    bp = jnp.stack([(r - ch < w) for w in POOL_WINDOWS]).astype(BF16)
    return bc, bp


def _inproj_kernel(x_ref, g_ref, w_ref, q_ref, k_ref, v_ref, u_ref, *, d):
    h = _rms(x_ref[...], g_ref[...]).astype(BF16)
    q_ref[...] = (_dot(h, w_ref[:, 0:d]) * (HEAD_DIM ** -0.5)).astype(BF16)
    k_ref[...] = _dot(h, w_ref[:, d:2 * d])
    v_ref[...] = _dot(h, w_ref[:, 2 * d:3 * d])
    u_ref[...] = _dot(h, w_ref[:, 3 * d:])


def _inproj(x, g, w_bf, tile):
    n, d = x.shape
    wu = w_bf.shape[1] - 3 * d
    row = lambda w: pl.BlockSpec((tile, w), lambda i: (i, 0))
    return pl.pallas_call(
        functools.partial(_inproj_kernel, d=d),
        grid=(n // tile,),
        in_specs=[row(d), _const_spec((1, d)), _const_spec(w_bf.shape)],
        out_specs=[row(d), row(d), row(d), row(wu)],
        out_shape=[jax.ShapeDtypeStruct((n, d), BF16), jax.ShapeDtypeStruct((n, d), F32),
                   jax.ShapeDtypeStruct((n, d), F32), jax.ShapeDtypeStruct((n, wu), F32)],
        compiler_params=_cparams("arbitrary"),
        name="inproj",
    )(x, g.reshape(1, d), w_bf)


def _head_norm(o, g, lam_init):
    return _rms(o, g) * (1.0 - lam_init)


def _lane_chunks(x):
    return [x[:, c * LANES:(c + 1) * LANES] for c in range(x.shape[1] // LANES)]


N_NEAR = 2


def _pattn_kernel(lam_ref, q_ref, k_ref, v_ref, bias_ref, g_ref, o_ref, qz_ref, kb_ref, vb_ref,
                  mxn_ref, mxf_ref, mbn_ref, mbf_ref, ls_ref, acc_ref, *s_refs, tq, nq, lam_init):
    s_len = q_ref.shape[0]
    near = N_NEAR * tq
    kb_ref[...] = k_ref[...].astype(BF16)
    vb_ref[...] = v_ref[...].astype(BF16)
    q = q_ref[...]
    lane = lax.broadcasted_iota(jnp.int32, q.shape, 1)
    zero = jnp.zeros_like(q)
    qz_ref[0] = jnp.where(lane < HEAD_DIM, q, zero)
    qz_ref[1] = jnp.where(lane >= HEAD_DIM, q, zero)
    c_far = bias_ref[N_NEAR][0:1, 0:1]
    mxn_ref[...] = jnp.full(mxn_ref.shape, NEG_INF, F32)
    mxf_ref[...] = jnp.full(mxf_ref.shape, NEG_INF, F32)

    def row_parts(kb):
        r0 = kb * tq
        parts = [(r0 + d * tq, r0 + (d + 1) * tq, True) for d in range(N_NEAR) if r0 + d * tq < s_len]
        if r0 + near < s_len:
            parts.append((r0 + near, s_len, False))
        return parts

    for kb in range(nq):
        r0 = kb * tq
        keys = kb_ref[r0:r0 + tq, :]
        for m in range(2):
            s = _dot_nt(qz_ref[m, r0:, :], keys)
            for lo, hi, is_near in row_parts(kb):
                t = s[lo - r0:hi - r0]
                if is_near:
                    t = t + bias_ref[(lo - r0) // tq]
                mx_ref = mxn_ref if is_near else mxf_ref
                mx_ref[m, lo:hi] = functools.reduce(jnp.maximum, _lane_chunks(t), mx_ref[m, lo:hi])
                s_refs[kb][m, lo - r0:hi - r0] = t

    mrow = jnp.maximum(jnp.max(mxn_ref[...], axis=2, keepdims=True),
                       jnp.max(mxf_ref[...], axis=2, keepdims=True) + c_far)
    mbn_ref[...] = jnp.broadcast_to(mrow, mbn_ref.shape)
    mbf_ref[...] = jnp.broadcast_to(mrow - c_far, mbf_ref.shape)
    ls_ref[...] = jnp.zeros(ls_ref.shape, F32)
    acc_ref[...] = jnp.zeros(acc_ref.shape, F32)

    for kb in range(nq):
        r0 = kb * tq
        vals = vb_ref[r0:r0 + tq, :]
        for m in range(2):
            for lo, hi, is_near in row_parts(kb):
                mb = (mbn_ref if is_near else mbf_ref)[m, lo:hi]
                p = [jnp.exp(c - mb) for c in _lane_chunks(s_refs[kb][m, lo - r0:hi - r0])]
                ls_ref[m, lo:hi] += functools.reduce(jnp.add, p)
                acc_ref[m, lo:hi] += _dot(jnp.concatenate(p, axis=1).astype(BF16), vals)

    o = acc_ref[...] / jnp.sum(ls_ref[...], axis=2, keepdims=True)
    o = o[0] - lam_ref[0, 0] * o[1]
    o_ref[...] = _head_norm(o, g_ref[...], lam_init).astype(o_ref.dtype)


def _prompt_attention(lam, q, k, v, bias_tiles, subln_g, lam_init, tq):
    b, s, d = q.shape
    nq = s // tq
    assert tq + 1 >= MAX_DISTANCE and bias_tiles.shape[1] == N_NEAR + 1
    seq = pl.BlockSpec((None, s, V_DIM), lambda h, bi: (bi, 0, h))
    stat = pltpu.VMEM((2, s, LANES), F32)
    return pl.pallas_call(
        functools.partial(_pattn_kernel, tq=tq, nq=nq, lam_init=lam_init),
        grid=(N_HEADS, b),
        in_specs=[pl.BlockSpec(memory_space=pltpu.SMEM), seq, seq, seq,
                  pl.BlockSpec((None, N_NEAR + 1, tq, tq), lambda h, bi: (h, 0, 0, 0)),
                  _const_spec((1, V_DIM))],
        out_specs=seq,
        out_shape=jax.ShapeDtypeStruct((b, s, d), BF16),
        scratch_shapes=[pltpu.VMEM((2, s, V_DIM), BF16), pltpu.VMEM((s, V_DIM), BF16), pltpu.VMEM((s, V_DIM), BF16),
                        stat, stat, stat, stat, stat, pltpu.VMEM((2, s, V_DIM), F32)]
                       + [pltpu.VMEM((2, s - kb * tq, tq), F32) for kb in range(nq)],
        compiler_params=_cparams("arbitrary", "arbitrary"),
        name="prompt_attn",
    )(lam, q, k, v, bias_tiles, subln_g.reshape(1, V_DIM))


def _dattn_kernel(pt_ref, lam_ref, q_ref, *refs, page, lam_init):
    del pt_ref
    npg = PAGES_PER_STEP
    k_refs, v_refs = refs[:npg], refs[npg:2 * npg]
    bias_ref, kn_ref, vn_ref, biasn_ref, g_ref, o_ref, m_ref, l_ref, acc_ref = refs[2 * npg:]
    g_idx, b = pl.program_id(0), pl.program_id(1)
    hr = m_ref.shape[1] // N_HEADS
    half = hr // 2

    @pl.when(g_idx == 0)
    def _():
        m_ref[b] = jnp.full(m_ref.shape[1:], NEG_INF, F32)
        l_ref[b] = jnp.zeros(l_ref.shape[1:], F32)
        acc_ref[b] = jnp.zeros(acc_ref.shape[1:], F32)

    def head_rows(ref, h):
        return ref[pl.ds(h, page, stride=N_HEADS), :].astype(BF16)

    def update(s, v_blocks):
        m_old = m_ref[b]
        m_new = jnp.maximum(m_old, jnp.max(s, axis=1, keepdims=True))
        alpha = jnp.exp(m_old - m_new)
        p = jnp.exp(s - m_new)
        l_ref[b] = alpha * l_ref[b] + jnp.sum(p, axis=1, keepdims=True)
        m_ref[b] = m_new
        pb = p.astype(BF16)
        for h in range(N_HEADS):
            rs = slice(h * hr, (h + 1) * hr)
            pv = None
            for i, v in enumerate(v_blocks[h]):
                t = _dot(pb[rs, i * page:(i + 1) * page], v)
                pv = t if pv is None else pv + t
            acc_ref[b, h] = alpha[rs] * acc_ref[b, h] + pv

    def page_scores(k_of_head):
        acc = None
        for pr in range(N_HEADS // 2):
            keys = jnp.concatenate([k_of_head(2 * pr), k_of_head(2 * pr + 1)], axis=1)
            t = _dot(keys, q_ref[pr])
            acc = t if acc is None else acc + t
        return acc.T

    def both_softmaxes(bias):
        return jnp.concatenate([bias[h * half:(h + 1) * half] for h in range(N_HEADS) for _ in range(2)], axis=0)

    s = jnp.concatenate([page_scores(lambda h, kr=kr: head_rows(kr, h)) for kr in k_refs], axis=1)
    update(s + both_softmaxes(bias_ref[...]), [[head_rows(vr, h) for vr in v_refs] for h in range(N_HEADS)])

    @pl.when(g_idx == pl.num_programs(0) - 1)
    def _():
        def pad_page(x):
            return jnp.concatenate([x, jnp.zeros((page - x.shape[0], x.shape[1]), x.dtype)], axis=0).astype(BF16)

        sn = page_scores(lambda h: pad_page(kn_ref[h]))
        update(sn + both_softmaxes(biasn_ref[...]), [[pad_page(vn_ref[h])] for h in range(N_HEADS)])
        linv = 1.0 / l_ref[b]
        g = g_ref[...]
        for h in range(N_HEADS):
            o = acc_ref[b, h] * linv[h * hr:(h + 1) * hr]
            o = o[:half] - lam_ref[0, 0] * o[half:]
            o_ref[b, :, h * V_DIM:(h + 1) * V_DIM] = _head_norm(o, g, lam_init)


def _decode_attention(page_table, lam, q_heads, cache_k, cache_v, bias_past, k_new, v_new, bias_new,
                      subln_g, lam_init, page):
    nb, n_pages = page_table.shape
    npg = PAGES_PER_STEP
    rows = q_heads.shape[-1]
    hr = rows // N_HEADS
    slots = hr // 2
    d = N_HEADS * V_DIM
    assert rows == LANES

    def page_spec(i):
        return pl.BlockSpec((None, page * N_HEADS, V_DIM), lambda g, b, pt: (pt[b, g * npg + i], 0, 0))

    seq = lambda *blk: pl.BlockSpec((None,) + blk, lambda g, b, pt: (b,) + (0,) * len(blk))
    in_specs = ([pl.BlockSpec(memory_space=pltpu.SMEM), seq(N_HEADS // 2, 2 * V_DIM, rows)]
                + [page_spec(i) for i in range(npg)] * 2
                + [pl.BlockSpec((rows // 2, npg * page), lambda g, b, pt: (0, g)),
                   seq(N_HEADS, slots, V_DIM), seq(N_HEADS, slots, V_DIM),
                   pl.BlockSpec((rows // 2, page), lambda g, b, pt: (0, 0)),
                   pl.BlockSpec((1, V_DIM), lambda g, b, pt: (0, 0))])
    return pl.pallas_call(
        functools.partial(_dattn_kernel, page=page, lam_init=lam_init),
        grid_spec=pltpu.PrefetchScalarGridSpec(
            num_scalar_prefetch=1,
            grid=(n_pages // npg, nb),
            in_specs=in_specs,
            out_specs=pl.BlockSpec((nb, slots, d), lambda g, b, pt: (0, 0, 0)),
            scratch_shapes=[pltpu.VMEM((nb, rows, 1), F32), pltpu.VMEM((nb, rows, 1), F32),
                            pltpu.VMEM((nb, N_HEADS, hr, V_DIM), F32)]),
        out_shape=jax.ShapeDtypeStruct((nb, slots, d), F32),
        compiler_params=_cparams("arbitrary", "arbitrary"),
        name="decode_attn",
    )(page_table, lam, q_heads, *([cache_k] * npg), *([cache_v] * npg), bias_past, k_new, v_new, bias_new,
      subln_g.reshape(1, V_DIM))


def _pool_kernel(u_ref, prev_ref, bc_ref, bp_ref, w_ref, scale_ref, o_ref, *, tile):
    i = pl.program_id(1)
    prev = jnp.where(i > 0, prev_ref[...], 0.0)
    _pool_rows(u_ref[...], prev, i * tile, bc_ref, bp_ref, w_ref, scale_ref, o_ref)


def _pool_mix(u, w_grp_bf, scale, tile):
    b, s, wdt = u.shape
    bc, bp = _pool_bands(min(tile, POOL_SUB))
    assert tile % bc.shape[1] == 0
    per_halo = tile // POOL_HALO
    return pl.pallas_call(
        functools.partial(_pool_kernel, tile=tile),
        grid=(b, s // tile),
        in_specs=[pl.BlockSpec((None, tile, wdt), lambda bi, i: (bi, i, 0)),
                  pl.BlockSpec((None, POOL_HALO, wdt), lambda bi, i: (bi, jnp.maximum(i * per_halo - 1, 0), 0)),
                  _const_spec(bc.shape), _const_spec(bp.shape), _const_spec(w_grp_bf.shape),
                  _const_spec((1, wdt))],
        out_specs=pl.BlockSpec((None, tile, wdt), lambda bi, i: (bi, i, 0)),
        out_shape=jax.ShapeDtypeStruct((b, s, wdt), BF16),
        compiler_params=_cparams("arbitrary", "arbitrary"),
        name="pool_mix",
    )(u, u, bc, bp, w_grp_bf, scale.reshape(1, wdt))


def _merge_kernel(x_ref, a_ref, p_ref, g_ref, wg_ref, wa_ref, wp_ref, wo_ref, o_ref, *, d):
    x = x_ref[...]
    h = _rms(x, g_ref[...]).astype(BF16)
    ga = jax.nn.sigmoid(_dot(h, wg_ref[:, :d]))
    merged = ga * _dot(a_ref[...], wa_ref[...])
    gp = jax.nn.sigmoid(_dot(h, wg_ref[:, d:]))
    merged += gp * _dot(p_ref[...], wp_ref[...])
    o_ref[...] = x + _dot(merged.astype(BF16), wo_ref[...])


def _merge(x, a, pooled, g, w_gate, w_br_attn, w_br_pool, w_out, tile):
    n, d = x.shape
    row = lambda w: pl.BlockSpec((tile, w), lambda i: (i, 0))
    return pl.pallas_call(
        functools.partial(_merge_kernel, d=d),
        grid=(n // tile,),
        in_specs=[row(d), row(a.shape[1]), row(pooled.shape[1]), _const_spec((1, d)),
                  _const_spec(w_gate.shape), _const_spec(w_br_attn.shape), _const_spec(w_br_pool.shape),
                  _const_spec(w_out.shape)],
        out_specs=row(d),
        out_shape=jax.ShapeDtypeStruct((n, d), F32),
        compiler_params=_cparams("arbitrary"),
        name="merge",
    )(x, a, pooled, g.reshape(1, d), w_gate, w_br_attn, w_br_pool, w_out)


def _memkv_kernel(x_ref, wk_ref, wv_ref, k_ref, v_ref):
    x = x_ref[...].astype(BF16)
    k_ref[...] = _dot(x, wk_ref[...])
    v_ref[...] = _dot(x, wv_ref[...])


def _memkv(x, wk, wv, tile):
    n, d = x.shape
    row = pl.BlockSpec((tile, d), lambda i: (i, 0))
    return pl.pallas_call(
        _memkv_kernel,
        grid=(n // tile,),
        in_specs=[row, _const_spec(wk.shape), _const_spec(wv.shape)],
        out_specs=[row, row],
        out_shape=[jax.ShapeDtypeStruct((n, d), F32)] * 2,
        compiler_params=_cparams("arbitrary"),
        name="mem_kv",
    )(x, wk, wv)


def _cross_kernel(x_ref, mk_ref, mv_ref, g_ref, wq_ref, wo_ref, o_ref, *, dh):
    x = x_ref[...]
    h = _rms(x, g_ref[...]).astype(BF16)
    q = (_dot(h, wq_ref[...]) * (dh ** -0.5)).astype(BF16)
    out = x
    for hh in range(X_HEADS):
        sl = slice(hh * dh, (hh + 1) * dh)
        s = _dot_nt(q[:, sl], mk_ref[:, sl].astype(BF16))
        p = jnp.exp(s - jnp.max(s, axis=1, keepdims=True))
        o = _dot(p.astype(BF16), mv_ref[:, sl].astype(BF16)) / jnp.sum(p, axis=1, keepdims=True)
        out += _dot(o.astype(BF16), wo_ref[sl, :])
    o_ref[...] = out


def _cross(x, mem_k, mem_v, g, w_cq, w_co, tile):
    b, s, d = x.shape
    m = mem_k.shape[1]
    xs = pl.BlockSpec((None, tile, d), lambda bi, i: (bi, i, 0))
    ms = pl.BlockSpec((None, m, d), lambda bi, i: (bi, 0, 0))
    return pl.pallas_call(
        functools.partial(_cross_kernel, dh=d // X_HEADS),
        grid=(b, s // tile),
        in_specs=[xs, ms, ms, _const_spec((1, d)), _const_spec(w_cq.shape), _const_spec(w_co.shape)],
        out_specs=xs,
        out_shape=jax.ShapeDtypeStruct((b, s, d), F32),
        compiler_params=_cparams("arbitrary", "arbitrary"),
        name="cross_attn",
    )(x, mem_k, mem_v, g.reshape(1, d), w_cq, w_co)


def _router_gates(h, wr_split, br):
    h_hi, h_lo = _split_bf16(h)
    parts = _dot(h_hi, wr_split) + _dot(h_lo, wr_split)
    logits = parts + pltpu.roll(parts, LANES // 2, axis=1) + br
    lane = lax.broadcasted_iota(jnp.int32, logits.shape, 1)
    big = jnp.int32(LANES)
    first = lambda mask: jnp.min(jnp.where(mask, lane, big), axis=1, keepdims=True)
    rmax = lambda x: jnp.max(x, axis=1, keepdims=True)

    gmask = (lane >= N_EXPERTS) & (lane < N_EXPERTS + N_GROUPS)
    gl = jnp.where(gmask, logits, NEG_INF)
    gmax = rmax(gl)
    grp = first(gl == gmax) - N_EXPERTS
    p_grp = 1.0 / jnp.sum(jnp.exp(gl - gmax), axis=1, keepdims=True)

    lo = grp * EXPERTS_PER_GROUP
    el = jnp.where((lane >= lo) & (lane < lo + EXPERTS_PER_GROUP), logits, NEG_INF)
    v1 = rmax(el)
    i1 = first(el == v1)
    el2 = jnp.where(lane == i1, NEG_INF, el)
    v2 = rmax(el2)
    i2 = first(el2 == v2)
    t = jnp.exp(v2 - v1)
    w1 = p_grp / (1.0 + t)
    return jnp.where(lane == i1, w1, 0.0) + jnp.where(lane == i2, w1 * t, 0.0), grp


def _split3_bf16(x):
    a = x.astype(BF16)
    r = x - a.astype(F32)
    b = r.astype(BF16)
    return a, b, (r - b.astype(F32)).astype(BF16)


def _moe_kernel(x_ref, g_ref, wr_ref, br_ref, w1_ref, w3_ref, w2_ref, gf_ref, o_ref,
                hs_ref, gs_ref, ys_ref, pt_ref, meta_ref, *, tile, tp, n_sub):
    sg = pl.program_id(1)
    per_step = EXPERTS_PER_GROUP // n_sub

    @pl.when(sg == 0)
    def _route():
        h = _rms(x_ref[...], g_ref[...])
        gates, grp = _router_gates(h, wr_ref[...], br_ref[...])
        lane = lax.broadcasted_iota(jnp.int32, gates.shape, 1)
        member = lane == grp
        r = lax.broadcasted_iota(jnp.int32, (tile, tile), 0)
        c = lax.broadcasted_iota(jnp.int32, (tile, tile), 1)
        tri = jnp.where(r >= c, 1.0, 0.0).astype(BF16)
        csum = _dot(tri, jnp.where(member, 1.0, 0.0).astype(BF16))
        cnt = csum[tile - 1:tile, :]
        padded = jnp.ceil(cnt * (1.0 / BF16_ROWS)) * BF16_ROWS
        lane_row = lane[0:1, :]
        off = jnp.zeros_like(cnt)
        run = jnp.zeros((1, 1), F32)
        for gg in range(N_GROUPS):
            off = jnp.where(lane_row == gg, run, off)
            run = run + padded[:, gg:gg + 1]
        pos = jnp.sum(jnp.where(member, off + csum - 1.0, 0.0), axis=1, keepdims=True)
        pos_row = jnp.transpose(jnp.broadcast_to(pos, (tile, LANES)))[0:1, :]
        pt = jnp.where(lax.broadcasted_iota(jnp.int32, (tile, tp), 1) == pos.astype(jnp.int32), 1.0, 0.0)
        p = jnp.where(lax.broadcasted_iota(jnp.int32, (tp, tile), 0) == pos_row.astype(jnp.int32), 1.0, 0.0)
        p = p.astype(BF16)
        pt_ref[...] = pt.astype(BF16)
        hs_ref[0:tp, :] = _dot(p, h.astype(BF16)).astype(BF16)
        hs_ref[tp:, :] = jnp.zeros((hs_ref.shape[0] - tp, hs_ref.shape[1]), BF16)
        grel = gates
        for gg in range(1, N_GROUPS):
            grel = jnp.where(grp == gg, pltpu.roll(gates, LANES - gg * EXPERTS_PER_GROUP, axis=1), grel)
        g3 = [t.astype(F32) for t in _split3_bf16(grel)]
        packed = g3[0] + pltpu.roll(g3[1], EXPERTS_PER_GROUP, axis=1) + pltpu.roll(g3[2], 2 * EXPERTS_PER_GROUP, axis=1)
        sorted_g = _dot(p, packed.astype(BF16))
        gs_ref[0:tp, :] = (sorted_g + pltpu.roll(sorted_g, LANES - EXPERTS_PER_GROUP, axis=1)
                           + pltpu.roll(sorted_g, LANES - 2 * EXPERTS_PER_GROUP, axis=1))
        gs_ref[tp:, :] = jnp.zeros((gs_ref.shape[0] - tp, gs_ref.shape[1]), F32)
        ys_ref[...] = jnp.zeros(ys_ref.shape, F32)
        for gg in range(N_GROUPS):
            pick = lambda v: jnp.sum(jnp.where(lane_row == gg, v, 0.0)).astype(jnp.int32)
            meta_ref[0, gg] = pick(off)
            meta_ref[1, gg] = pick(cnt)

    grp_id = sg // n_sub
    e0 = (sg % n_sub) * per_step
    start = meta_ref[0, grp_id]
    cnt = meta_ref[1, grp_id]
    lane8 = lax.broadcasted_iota(jnp.int32, (MOE_BLOCK, LANES), 1)

    def block(i, carry):
        r0 = pl.multiple_of(start + i * MOE_BLOCK, BF16_ROWS)
        rows = hs_ref[pl.ds(r0, MOE_BLOCK), :]
        ridx = r0 + lax.broadcasted_iota(jnp.int32, (MOE_BLOCK, 1), 0)
        gts = jnp.where(ridx < start + cnt, gs_ref[pl.ds(r0, MOE_BLOCK), :], 0.0)
        y = None
        for j in range(per_step):
            gate = jnp.sum(jnp.where(lane8 == e0 + j, gts, 0.0), axis=1, keepdims=True)
            h1 = _dot(rows, w1_ref[j])
            h3 = _dot(rows, w3_ref[j])
            act = (h1 * jax.nn.sigmoid(h1)) * h3 * gate
            t = _dot(act.astype(BF16), w2_ref[j])
            y = t if y is None else y + t
        ys_ref[pl.ds(r0, MOE_BLOCK), :] += y
        return carry

    lax.fori_loop(0, (cnt + MOE_BLOCK - 1) // MOE_BLOCK, block, 0)

    @pl.when(sg == pl.num_programs(1) - 1)
    def _finish():
        pt = pt_ref[...]
        y_hi, y_lo = _split_bf16(ys_ref[0:tp, :])
        o_ref[...] = _rms(x_ref[...] + (_dot(pt, y_hi) + _dot(pt, y_lo)), gf_ref[...])


def _moe_final(x, g, wr_split, br, w1, w3, w2, g_final, tile, n_sub):
    n, d = x.shape
    n_e, _, f = w1.shape
    per_step = EXPERTS_PER_GROUP // n_sub
    tp = tile + LANES
    rows = tp + MOE_BLOCK
    row = pl.BlockSpec((tile, d), lambda i, s: (i, 0))
    cst = lambda shape: pl.BlockSpec(shape, lambda i, s: (0,) * len(shape))
    return pl.pallas_call(
        functools.partial(_moe_kernel, tile=tile, tp=tp, n_sub=n_sub),
        grid=(n // tile, N_GROUPS * n_sub),
        in_specs=[row, cst((1, d)), cst(wr_split.shape), cst((1, LANES)),
                  pl.BlockSpec((per_step, d, f), lambda i, s: (s, 0, 0)),
                  pl.BlockSpec((per_step, d, f), lambda i, s: (s, 0, 0)),
                  pl.BlockSpec((per_step, f, d), lambda i, s: (s, 0, 0)),
                  cst((1, d))],
        out_specs=row,
        out_shape=jax.ShapeDtypeStruct((n, d), F32),
        scratch_shapes=[pltpu.VMEM((rows, d), BF16), pltpu.VMEM((rows, LANES), F32), pltpu.VMEM((rows, d), F32),
                        pltpu.VMEM((tile, tp), BF16), pltpu.SMEM((2, N_GROUPS), jnp.int32)],
        compiler_params=_cparams("arbitrary", "arbitrary"),
        name="moe_final",
    )(x, g.reshape(1, d), wr_split, br, w1, w3, w2, g_final.reshape(1, d))


def _t5_bucket(rel):
    n = jnp.maximum(-rel, 0)
    max_exact = NUM_BUCKETS // 2
    nf = jnp.maximum(n, max_exact).astype(F32)
    large = max_exact + (jnp.log(nf / max_exact) / math.log(MAX_DISTANCE / max_exact)
                         * (NUM_BUCKETS - max_exact)).astype(jnp.int32)
    large = jnp.minimum(large, NUM_BUCKETS - 1)
    return jnp.where(n < max_exact, n, large)


def _distance_bias(rel_bias, n_dist, descending=False):
    dist = jnp.arange(n_dist, dtype=jnp.int32)
    bucket = _t5_bucket(-(n_dist - 1 - dist if descending else dist))
    onehot = (bucket[:, None] == jnp.arange(NUM_BUCKETS, dtype=jnp.int32)[None, :]).astype(F32)
    return jnp.dot(onehot, rel_bias.astype(F32), precision=lax.Precision.HIGHEST).T


def _prompt_bias_tiles(rel_bias, tq, nd):
    p = 2 * tq
    table = _distance_bias(rel_bias, nd * tq)
    ext = jnp.pad(table, ((0, 0), (tq, 1)), constant_values=NEG_INF)
    c = jnp.stack([jnp.concatenate([ext[:, d * tq + 1:d * tq + tq + 1][:, ::-1],
                                    ext[:, d * tq + tq + 1:d * tq + p + 1][:, ::-1]], axis=1)
                   for d in range(nd)], axis=1)
    m = jnp.tile(c, (1, 1, tq))[:, :, :tq * (p - 1)].reshape(N_HEADS, nd, tq, p - 1)
    return m[..., :tq]


def _decode_bias(rel_bias, past_len, slots, n_new, page):
    n_dist = past_len + slots
    table = _distance_bias(rel_bias, n_dist, descending=True)
    past = jnp.stack([table[:, slots - 1 - j:slots - 1 - j + past_len] for j in range(slots)], axis=1)
    jq = jnp.arange(slots, dtype=jnp.int32)[:, None]
    jk = jnp.arange(page, dtype=jnp.int32)[None, :]
    visible = (jk <= jq) & (jk < n_new)
    near = table[:, n_dist - slots:][:, ::-1]
    new = jnp.where(visible[None], near[:, jnp.clip(jq - jk, 0, slots - 1)], NEG_INF)
    return past.reshape(N_HEADS * slots, -1), new.reshape(N_HEADS * slots, -1)


def _head_queries(q, n_q, slots):
    nb = q.shape[0] // n_q
    qh = jnp.transpose(q.reshape(nb, n_q, N_HEADS, V_DIM), (0, 2, 1, 3))
    qh = jnp.pad(qh, ((0, 0), (0, 0), (0, slots - n_q), (0, 0)))
    first = jnp.arange(V_DIM) < HEAD_DIM
    zero = jnp.zeros((), q.dtype)
    return jnp.concatenate([jnp.where(first, qh, zero), jnp.where(first, zero, qh)], axis=2)


def _head_pair_queries(q, n_q, slots):
    qh = _head_queries(q, n_q, slots)
    hr = qh.shape[2]
    qt = jnp.swapaxes(qh, 2, 3)
    cols = [jnp.pad(qt[:, h], ((0, 0), (0, 0), (h * hr, (N_HEADS - 1 - h) * hr))) for h in range(N_HEADS)]
    return jnp.stack([jnp.concatenate(cols[2 * pr:2 * pr + 2], axis=1) for pr in range(N_HEADS // 2)], axis=1)


def _head_major(t, n_q, slots):
    nb = t.shape[0] // n_q
    th = jnp.transpose(t.reshape(nb, n_q, N_HEADS, V_DIM), (0, 2, 1, 3))
    return jnp.pad(th, ((0, 0), (0, 0), (0, slots - n_q), (0, 0)))


def _pick_tile(n, pref):
    return pref if n % pref == 0 else n


def kernel(x_prompt, x_sample, mem_prompt, cache_k, cache_v, page_table, cache_mem_k, cache_mem_v, state_pool, rel_bias, norm_mix_g, w_in, lambda_q1, lambda_k1, lambda_q2, lambda_k2, subln_g, w_pool_grp, pool_scale, w_br_attn, w_br_pool, w_gate, w_out, norm_cross_g, w_cq, w_ck, w_cv, w_co, norm_ffn_g, w_router_grp, b_router_grp, w_router_exp, b_router_exp, w_exp_gate, w_exp_up, w_exp_down, norm_final_g):
    depth = w_in.shape[0]
    assert depth == 1, "single-layer step"
    l = 0
    b_p, s_p, d = x_prompt.shape
    b_s, s_s, _ = x_sample.shape
    n_p, n_s = b_p * s_p, b_s * s_s
    n_mem = mem_prompt.shape[1]
    page = cache_k.shape[2]
    past_len = page_table.shape[1] * page
    pool_w = state_pool.shape[-1]

    lam_init = 0.8 - 0.6 * math.exp(-0.3 * l)
    lam = (jnp.exp(jnp.sum(lambda_q1[l].astype(F32) * lambda_k1[l].astype(F32)))
           - jnp.exp(jnp.sum(lambda_q2[l].astype(F32) * lambda_k2[l].astype(F32))) + lam_init).reshape(1, 1)

    bf = lambda w: w.astype(BF16)
    w_in_bf, w_gate_bf, w_ba_bf, w_bp_bf, w_out_bf = bf(w_in[l]), bf(w_gate[l]), bf(w_br_attn[l]), bf(w_br_pool[l]), bf(w_out[l])
    w_cq_bf, w_ck_bf, w_cv_bf, w_co_bf = bf(w_cq[l]), bf(w_ck[l]), bf(w_cv[l]), bf(w_co[l])
    w_grp_bf = bf(w_pool_grp[l])
    w1_bf, w3_bf, w2_bf = bf(w_exp_gate[l]), bf(w_exp_up[l]), bf(w_exp_down[l])
    wr = jnp.concatenate([w_router_exp[l], w_router_grp[l]], axis=1).astype(F32)
    n_router = wr.shape[1]
    assert n_router <= LANES // 2
    wr_hi = wr.astype(BF16)
    wr_lo = (wr - wr_hi.astype(F32)).astype(BF16)
    lane_pad = lambda t, lo: jnp.pad(t, ((0, 0), (lo, LANES - lo - n_router)))
    wr_split = lane_pad(wr_hi, 0) + lane_pad(wr_lo, LANES // 2)
    br = jnp.concatenate([b_router_exp[l], b_router_grp[l]]).astype(F32)
    br = jnp.pad(br, (0, LANES - br.shape[0])).reshape(1, LANES)

    def token_tail(x_tok, a, pooled, mem_k, mem_v, nb, tile, cross_tile):
        x1 = _merge(x_tok, a, pooled, norm_mix_g[l], w_gate_bf, w_ba_bf, w_bp_bf, w_out_bf, tile)
        x1 = x1.reshape(nb, -1, d)
        seq = x1.shape[1]
        x1 = jnp.pad(x1, ((0, 0), (0, -seq % BF16_ROWS), (0, 0)))
        x2 = _cross(x1, mem_k, mem_v, norm_cross_g[l], w_cq_bf, w_co_bf, max(cross_tile, BF16_ROWS))[:, :seq]
        n_tok = x_tok.shape[0]
        return _moe_final(x2.reshape(-1, d), norm_ffn_g[l], wr_split, br, w1_bf, w3_bf, w2_bf,
                          norm_final_g, _pick_tile(n_tok, 1024), 2)

    tile_p = _pick_tile(n_p, 512)
    tq = _pick_tile(s_p, 256)
    q, k, v, u = _inproj(x_prompt.reshape(n_p, d), norm_mix_g[l], w_in_bf, tile_p)
    bias_tiles = _prompt_bias_tiles(rel_bias, tq, N_NEAR + 1)
    a = _prompt_attention(lam, q.reshape(b_p, s_p, d), k.reshape(b_p, s_p, d), v.reshape(b_p, s_p, d),
                          bias_tiles, subln_g[l], lam_init, tq)
    u3 = u.reshape(b_p, s_p, pool_w)
    pooled = _pool_mix(u3, w_grp_bf, pool_scale[l], _pick_tile(s_p, 256))
    mem_k, mem_v = _memkv(mem_prompt.reshape(b_p * n_mem, d), w_ck_bf, w_cv_bf, _pick_tile(b_p * n_mem, 512))
    mem_k, mem_v = mem_k.reshape(b_p, n_mem, d), mem_v.reshape(b_p, n_mem, d)
    y_prompt = token_tail(x_prompt.reshape(n_p, d), a.reshape(n_p, d), pooled.reshape(n_p, pool_w),
                          mem_k, mem_v, b_p, tile_p, _pick_tile(s_p, 512)).reshape(b_p, s_p, d)
    k_prompt = k.reshape(1, b_p, s_p, N_HEADS, V_DIM)
    v_prompt = v.reshape(1, b_p, s_p, N_HEADS, V_DIM)
    pool_prompt = u3[:, -POOL_CTX:][None]
    x_heads_dim = d // X_HEADS
    mem_k_prompt = mem_k.reshape(1, b_p, n_mem, X_HEADS, x_heads_dim)
    mem_v_prompt = mem_v.reshape(1, b_p, n_mem, X_HEADS, x_heads_dim)

    qs, ks, vs, us = _inproj(x_sample.reshape(n_s, d), norm_mix_g[l], w_in_bf, n_s)
    slots = -(-s_s // 8) * 8
    bias_past, bias_new = _decode_bias(rel_bias, past_len, slots, s_s, page)
    paged = lambda c: c.reshape(-1, page * N_HEADS, V_DIM)
    a_s = _decode_attention(page_table.astype(jnp.int32), lam, _head_pair_queries(qs, s_s, slots),
                            paged(cache_k), paged(cache_v), bias_past,
                            _head_major(ks, s_s, slots), _head_major(vs, s_s, slots), bias_new,
                            subln_g[l], lam_init, page)[:, :s_s]
    ctx = state_pool[l].astype(F32)
    full = jnp.concatenate([ctx, us.reshape(b_s, s_s, pool_w)], axis=1)
    slot = full.shape[1] + (-full.shape[1] % (2 * BF16_ROWS))
    stream = jnp.pad(full, ((0, 0), (0, slot - full.shape[1]), (0, 0))).reshape(1, b_s * slot, pool_w)
    pooled_s = _pool_mix(stream, w_grp_bf, pool_scale[l], _pick_tile(b_s * slot, 256))
    pooled_s = pooled_s.reshape(b_s, slot, pool_w)[:, POOL_CTX:POOL_CTX + s_s]
    y_sample = token_tail(x_sample.reshape(n_s, d), a_s.reshape(n_s, d).astype(BF16), pooled_s.reshape(n_s, pool_w),
                          cache_mem_k[l].reshape(b_s, n_mem, d), cache_mem_v[l].reshape(b_s, n_mem, d),
                          b_s, n_s, s_s).reshape(b_s, s_s, d)
    k_sample = ks.reshape(1, b_s, s_s, N_HEADS, V_DIM)
    v_sample = vs.reshape(1, b_s, s_s, N_HEADS, V_DIM)
    pool_sample = full[:, -POOL_CTX:][None]

    return (y_prompt, y_sample, k_prompt, v_prompt, pool_prompt, mem_k_prompt, mem_v_prompt,
            k_sample, v_sample, pool_sample)
```

```python
import functools
import math

import jax
import jax.numpy as jnp
from jax import lax
from jax.experimental import pallas as pl
from jax.experimental.pallas import tpu as pltpu

F32 = jnp.float32
BF16 = jnp.bfloat16

RMS_EPS = 1e-6
N_HEADS = 8
HEAD_DIM = 64
V_DIM = 2 * HEAD_DIM
NUM_BUCKETS = 32
MAX_DISTANCE = 128
POOL_WINDOWS = (2, 4, 8, 16)
POOL_GROUP_DIM = 128
POOL_CTX = max(POOL_WINDOWS) - 1
X_HEADS = 4
N_GROUPS = 4
EXPERTS_PER_GROUP = 8
N_EXPERTS = N_GROUPS * EXPERTS_PER_GROUP
LANES = 128
BF16_ROWS = 16
VMEM_LIMIT = 56 * 1024 * 1024
PAGES_PER_STEP = 16
MOE_BLOCK = 288
NEG_INF = float("-inf")


def _cparams(*sem):
    return pltpu.CompilerParams(dimension_semantics=sem, vmem_limit_bytes=VMEM_LIMIT)


def _rms(x, g):
    return x * lax.rsqrt(jnp.mean(x * x, axis=-1, keepdims=True) + RMS_EPS) * g


def _dot(a, b):
    return jnp.dot(a, b, preferred_element_type=F32)


def _dot_nt(a, b):
    return lax.dot_general(a, b, (((1,), (1,)), ((), ())), preferred_element_type=F32)


def _const_spec(shape):
    zeros = (0,) * len(shape)
    return pl.BlockSpec(shape, lambda *_: zeros)


POOL_HALO = 16


def _split_bf16(x):
    hi = x.astype(BF16)
    return hi, (x - hi.astype(F32)).astype(BF16)


def _pool_rows(u, prev, pos0, w_ref, scale_ref, o_ref):
    tile = u.shape[0]
    pos = pos0 + lax.broadcasted_iota(jnp.int32, (tile, 1), 0)
    for g, w in enumerate(POOL_WINDOWS):
        sl = slice(g * POOL_GROUP_DIM, (g + 1) * POOL_GROUP_DIM)
        ug = u[:, sl]
        tot = jnp.concatenate([prev[:, sl], ug], axis=0)
        span = 1
        while span < w:
            tot = tot + pltpu.roll(tot, span, axis=0)
            span *= 2
        cnt = jnp.minimum(pos + 1, w).astype(F32)
        pooled = tot[POOL_HALO:] / cnt - ug
        mixed = _dot(pooled.astype(BF16), w_ref[g]) * scale_ref[:, sl]
        o_ref[:, sl] = mixed.astype(o_ref.dtype)


def _inproj_kernel(x_ref, g_ref, w_ref, q_ref, k_ref, v_ref, u_ref, *, d):
    h = _rms(x_ref[...], g_ref[...]).astype(BF16)
    q_ref[...] = (_dot(h, w_ref[:, 0:d]) * (HEAD_DIM ** -0.5)).astype(BF16)
    k_ref[...] = _dot(h, w_ref[:, d:2 * d])
    v_ref[...] = _dot(h, w_ref[:, 2 * d:3 * d])
    u_ref[...] = _dot(h, w_ref[:, 3 * d:])


def _inproj(x, g, w_bf, tile):
    n, d = x.shape
    wu = w_bf.shape[1] - 3 * d
    row = lambda w: pl.BlockSpec((tile, w), lambda i: (i, 0))
    return pl.pallas_call(
        functools.partial(_inproj_kernel, d=d),
        grid=(n // tile,),
        in_specs=[row(d), _const_spec((1, d)), _const_spec(w_bf.shape)],
        out_specs=[row(d), row(d), row(d), row(wu)],
        out_shape=[jax.ShapeDtypeStruct((n, d), BF16), jax.ShapeDtypeStruct((n, d), F32),
                   jax.ShapeDtypeStruct((n, d), F32), jax.ShapeDtypeStruct((n, wu), F32)],
        compiler_params=_cparams("arbitrary"),
        name="inproj",
    )(x, g.reshape(1, d), w_bf)


def _head_norm(o, g, lam_init):
    return _rms(o, g) * (1.0 - lam_init)


def _lane_chunks(x):
    return [x[:, c * LANES:(c + 1) * LANES] for c in range(x.shape[1] // LANES)]


N_NEAR = 2


def _pattn_kernel(lam_ref, q_ref, k_ref, v_ref, bias_ref, g_ref, o_ref, qz_ref, kb_ref, vb_ref,
                  mxn_ref, mxf_ref, mbn_ref, mbf_ref, ls_ref, acc_ref, *s_refs, tq, nq, lam_init):
    s_len = q_ref.shape[0]
    near = N_NEAR * tq
    kb_ref[...] = k_ref[...].astype(BF16)
    vb_ref[...] = v_ref[...].astype(BF16)
    q = q_ref[...]
    lane = lax.broadcasted_iota(jnp.int32, q.shape, 1)
    zero = jnp.zeros_like(q)
    qz_ref[0] = jnp.where(lane < HEAD_DIM, q, zero)
    qz_ref[1] = jnp.where(lane >= HEAD_DIM, q, zero)
    c_far = bias_ref[N_NEAR][0:1, 0:1]
    mxn_ref[...] = jnp.full(mxn_ref.shape, NEG_INF, F32)
    mxf_ref[...] = jnp.full(mxf_ref.shape, NEG_INF, F32)

    def row_parts(kb):
        r0 = kb * tq
        parts = [(r0 + d * tq, r0 + (d + 1) * tq, True) for d in range(N_NEAR) if r0 + d * tq < s_len]
        if r0 + near < s_len:
            parts.append((r0 + near, s_len, False))
        return parts

    for kb in range(nq):
        r0 = kb * tq
        keys = kb_ref[r0:r0 + tq, :]
        for m in range(2):
            s = _dot_nt(qz_ref[m, r0:, :], keys)
            for lo, hi, is_near in row_parts(kb):
                t = s[lo - r0:hi - r0]
                if is_near:
                    t = t + bias_ref[(lo - r0) // tq]
                mx_ref = mxn_ref if is_near else mxf_ref
                mx_ref[m, lo:hi] = functools.reduce(jnp.maximum, _lane_chunks(t), mx_ref[m, lo:hi])
                s_refs[kb][m, lo - r0:hi - r0] = t

    mrow = jnp.maximum(jnp.max(mxn_ref[...], axis=2, keepdims=True),
                       jnp.max(mxf_ref[...], axis=2, keepdims=True) + c_far)
    mbn_ref[...] = jnp.broadcast_to(mrow, mbn_ref.shape)
    mbf_ref[...] = jnp.broadcast_to(mrow - c_far, mbf_ref.shape)
    ls_ref[...] = jnp.zeros(ls_ref.shape, F32)
    acc_ref[...] = jnp.zeros(acc_ref.shape, F32)

    for kb in range(nq):
        r0 = kb * tq
        vals = vb_ref[r0:r0 + tq, :]
        for m in range(2):
            for lo, hi, is_near in row_parts(kb):
                mb = (mbn_ref if is_near else mbf_ref)[m, lo:hi]
                p = [jnp.exp(c - mb) for c in _lane_chunks(s_refs[kb][m, lo - r0:hi - r0])]
                ls_ref[m, lo:hi] += functools.reduce(jnp.add, p)
                acc_ref[m, lo:hi] += _dot(jnp.concatenate(p, axis=1).astype(BF16), vals)

    o = acc_ref[...] / jnp.sum(ls_ref[...], axis=2, keepdims=True)
    o = o[0] - lam_ref[0, 0] * o[1]
    o_ref[...] = _head_norm(o, g_ref[...], lam_init).astype(o_ref.dtype)


def _prompt_attention(lam, q, k, v, bias_tiles, subln_g, lam_init, tq):
    b, s, d = q.shape
    nq = s // tq
    assert tq + 1 >= MAX_DISTANCE and bias_tiles.shape[1] == N_NEAR + 1
    seq = pl.BlockSpec((None, s, V_DIM), lambda h, bi: (bi, 0, h))
    stat = pltpu.VMEM((2, s, LANES), F32)
    return pl.pallas_call(
        functools.partial(_pattn_kernel, tq=tq, nq=nq, lam_init=lam_init),
        grid=(N_HEADS, b),
        in_specs=[pl.BlockSpec(memory_space=pltpu.SMEM), seq, seq, seq,
                  pl.BlockSpec((None, N_NEAR + 1, tq, tq), lambda h, bi: (h, 0, 0, 0)),
                  _const_spec((1, V_DIM))],
        out_specs=seq,
        out_shape=jax.ShapeDtypeStruct((b, s, d), BF16),
        scratch_shapes=[pltpu.VMEM((2, s, V_DIM), BF16), pltpu.VMEM((s, V_DIM), BF16), pltpu.VMEM((s, V_DIM), BF16),
                        stat, stat, stat, stat, stat, pltpu.VMEM((2, s, V_DIM), F32)]
                       + [pltpu.VMEM((2, s - kb * tq, tq), F32) for kb in range(nq)],
        compiler_params=_cparams("arbitrary", "arbitrary"),
        name="prompt_attn",
    )(lam, q, k, v, bias_tiles, subln_g.reshape(1, V_DIM))


def _dattn_kernel(pt_ref, lam_ref, q_ref, *refs, page, lam_init):
    del pt_ref
    npg = PAGES_PER_STEP
    k_refs, v_refs = refs[:npg], refs[npg:2 * npg]
    bias_ref, kn_ref, vn_ref, biasn_ref, g_ref, o_ref, m_ref, l_ref, acc_ref = refs[2 * npg:]
    g_idx, b = pl.program_id(0), pl.program_id(1)
    hr = m_ref.shape[1] // N_HEADS
    half = hr // 2

    @pl.when(g_idx == 0)
    def _():
        m_ref[b] = jnp.full(m_ref.shape[1:], NEG_INF, F32)
        l_ref[b] = jnp.zeros(l_ref.shape[1:], F32)
        acc_ref[b] = jnp.zeros(acc_ref.shape[1:], F32)

    def head_rows(ref, h):
        return ref[pl.ds(h, page, stride=N_HEADS), :].astype(BF16)

    def update(s, v_blocks):
        m_old = m_ref[b]
        m_new = jnp.maximum(m_old, jnp.max(s, axis=1, keepdims=True))
        alpha = jnp.exp(m_old - m_new)
        p = jnp.exp(s - m_new)
        l_ref[b] = alpha * l_ref[b] + jnp.sum(p, axis=1, keepdims=True)
        m_ref[b] = m_new
        pb = p.astype(BF16)
        for h in range(N_HEADS):
            rs = slice(h * hr, (h + 1) * hr)
            pv = None
            for i, v in enumerate(v_blocks[h]):
                t = _dot(pb[rs, i * page:(i + 1) * page], v)
                pv = t if pv is None else pv + t
            acc_ref[b, h] = alpha[rs] * acc_ref[b, h] + pv

    def page_scores(k_of_head):
        acc = None
        for pr in range(N_HEADS // 2):
            keys = jnp.concatenate([k_of_head(2 * pr), k_of_head(2 * pr + 1)], axis=1)
            t = _dot(keys, q_ref[pr])
            acc = t if acc is None else acc + t
        return acc.T

    def both_softmaxes(bias):
        return jnp.concatenate([bias[h * half:(h + 1) * half] for h in range(N_HEADS) for _ in range(2)], axis=0)

    s = jnp.concatenate([page_scores(lambda h, kr=kr: head_rows(kr, h)) for kr in k_refs], axis=1)
    update(s + both_softmaxes(bias_ref[...]), [[head_rows(vr, h) for vr in v_refs] for h in range(N_HEADS)])

    @pl.when(g_idx == pl.num_programs(0) - 1)
    def _():
        def pad_page(x):
            return jnp.concatenate([x, jnp.zeros((page - x.shape[0], x.shape[1]), x.dtype)], axis=0).astype(BF16)

        sn = page_scores(lambda h: pad_page(kn_ref[h]))
        update(sn + both_softmaxes(biasn_ref[...]), [[pad_page(vn_ref[h])] for h in range(N_HEADS)])
        linv = 1.0 / l_ref[b]
        g = g_ref[...]
        for h in range(N_HEADS):
            o = acc_ref[b, h] * linv[h * hr:(h + 1) * hr]
            o = o[:half] - lam_ref[0, 0] * o[half:]
            o_ref[b, :, h * V_DIM:(h + 1) * V_DIM] = _head_norm(o, g, lam_init)


def _decode_attention(page_table, lam, q_heads, cache_k, cache_v, bias_past, k_new, v_new, bias_new,
                      subln_g, lam_init, page):
    nb, n_pages = page_table.shape
    npg = PAGES_PER_STEP
    rows = q_heads.shape[-1]
    hr = rows // N_HEADS
    slots = hr // 2
    d = N_HEADS * V_DIM
    assert rows == LANES

    def page_spec(i):
        return pl.BlockSpec((None, page * N_HEADS, V_DIM), lambda g, b, pt: (pt[b, g * npg + i], 0, 0))

    seq = lambda *blk: pl.BlockSpec((None,) + blk, lambda g, b, pt: (b,) + (0,) * len(blk))
    in_specs = ([pl.BlockSpec(memory_space=pltpu.SMEM), seq(N_HEADS // 2, 2 * V_DIM, rows)]
                + [page_spec(i) for i in range(npg)] * 2
                + [pl.BlockSpec((rows // 2, npg * page), lambda g, b, pt: (0, g)),
                   seq(N_HEADS, slots, V_DIM), seq(N_HEADS, slots, V_DIM),
                   pl.BlockSpec((rows // 2, page), lambda g, b, pt: (0, 0)),
                   pl.BlockSpec((1, V_DIM), lambda g, b, pt: (0, 0))])
    return pl.pallas_call(
        functools.partial(_dattn_kernel, page=page, lam_init=lam_init),
        grid_spec=pltpu.PrefetchScalarGridSpec(
            num_scalar_prefetch=1,
            grid=(n_pages // npg, nb),
            in_specs=in_specs,
            out_specs=pl.BlockSpec((nb, slots, d), lambda g, b, pt: (0, 0, 0)),
            scratch_shapes=[pltpu.VMEM((nb, rows, 1), F32), pltpu.VMEM((nb, rows, 1), F32),
                            pltpu.VMEM((nb, N_HEADS, hr, V_DIM), F32)]),
        out_shape=jax.ShapeDtypeStruct((nb, slots, d), F32),
        compiler_params=_cparams("arbitrary", "arbitrary"),
        name="decode_attn",
    )(page_table, lam, q_heads, *([cache_k] * npg), *([cache_v] * npg), bias_past, k_new, v_new, bias_new,
      subln_g.reshape(1, V_DIM))


def _pool_kernel(u_ref, prev_ref, w_ref, scale_ref, o_ref, *, tile):
    i = pl.program_id(1)
    prev = jnp.where(i > 0, prev_ref[...], 0.0)
    _pool_rows(u_ref[...], prev, i * tile, w_ref, scale_ref, o_ref)


def _pool_mix(u, w_grp_bf, scale, tile):
    b, s, wdt = u.shape
    assert max(POOL_WINDOWS) <= POOL_HALO and tile % POOL_HALO == 0
    per_halo = tile // POOL_HALO
    return pl.pallas_call(
        functools.partial(_pool_kernel, tile=tile),
        grid=(b, s // tile),
        in_specs=[pl.BlockSpec((None, tile, wdt), lambda bi, i: (bi, i, 0)),
                  pl.BlockSpec((None, POOL_HALO, wdt), lambda bi, i: (bi, jnp.maximum(i * per_halo - 1, 0), 0)),
                  _const_spec(w_grp_bf.shape), _const_spec((1, wdt))],
        out_specs=pl.BlockSpec((None, tile, wdt), lambda bi, i: (bi, i, 0)),
        out_shape=jax.ShapeDtypeStruct((b, s, wdt), BF16),
        compiler_params=_cparams("arbitrary", "arbitrary"),
        name="pool_mix",
    )(u, u, w_grp_bf, scale.reshape(1, wdt))


def _merge_kernel(x_ref, a_ref, p_ref, g_ref, wg_ref, wa_ref, wp_ref, wo_ref, o_ref, *, d):
    x = x_ref[...]
    h = _rms(x, g_ref[...]).astype(BF16)
    ga = jax.nn.sigmoid(_dot(h, wg_ref[:, :d]))
    merged = ga * _dot(a_ref[...], wa_ref[...])
    gp = jax.nn.sigmoid(_dot(h, wg_ref[:, d:]))
    merged += gp * _dot(p_ref[...], wp_ref[...])
    o_ref[...] = x + _dot(merged.astype(BF16), wo_ref[...])


def _merge(x, a, pooled, g, w_gate, w_br_attn, w_br_pool, w_out, tile):
    n, d = x.shape
    row = lambda w: pl.BlockSpec((tile, w), lambda i: (i, 0))
    return pl.pallas_call(
        functools.partial(_merge_kernel, d=d),
        grid=(n // tile,),
        in_specs=[row(d), row(a.shape[1]), row(pooled.shape[1]), _const_spec((1, d)),
                  _const_spec(w_gate.shape), _const_spec(w_br_attn.shape), _const_spec(w_br_pool.shape),
                  _const_spec(w_out.shape)],
        out_specs=row(d),
        out_shape=jax.ShapeDtypeStruct((n, d), F32),
        compiler_params=_cparams("arbitrary"),
        name="merge",
    )(x, a, pooled, g.reshape(1, d), w_gate, w_br_attn, w_br_pool, w_out)


def _memkv_kernel(x_ref, wk_ref, wv_ref, k_ref, v_ref):
    x = x_ref[...].astype(BF16)
    k_ref[...] = _dot(x, wk_ref[...])
    v_ref[...] = _dot(x, wv_ref[...])


def _memkv(x, wk, wv, tile):
    n, d = x.shape
    row = pl.BlockSpec((tile, d), lambda i: (i, 0))
    return pl.pallas_call(
        _memkv_kernel,
        grid=(n // tile,),
        in_specs=[row, _const_spec(wk.shape), _const_spec(wv.shape)],
        out_specs=[row, row],
        out_shape=[jax.ShapeDtypeStruct((n, d), F32)] * 2,
        compiler_params=_cparams("arbitrary"),
        name="mem_kv",
    )(x, wk, wv)


def _cross_kernel(x_ref, mk_ref, mv_ref, g_ref, wq_ref, wo_ref, o_ref, *, dh):
    x = x_ref[...]
    h = _rms(x, g_ref[...]).astype(BF16)
    q = (_dot(h, wq_ref[...]) * (dh ** -0.5)).astype(BF16)
    out = x
    for hh in range(X_HEADS):
        sl = slice(hh * dh, (hh + 1) * dh)
        s = _dot_nt(q[:, sl], mk_ref[:, sl].astype(BF16))
        p = jnp.exp(s - jnp.max(s, axis=1, keepdims=True))
        o = _dot(p.astype(BF16), mv_ref[:, sl].astype(BF16)) / jnp.sum(p, axis=1, keepdims=True)
        out += _dot(o.astype(BF16), wo_ref[sl, :])
    o_ref[...] = out


def _cross(x, mem_k, mem_v, g, w_cq, w_co, tile):
    b, s, d = x.shape
    m = mem_k.shape[1]
    xs = pl.BlockSpec((None, tile, d), lambda bi, i: (bi, i, 0))
    ms = pl.BlockSpec((None, m, d), lambda bi, i: (bi, 0, 0))
    return pl.pallas_call(
        functools.partial(_cross_kernel, dh=d // X_HEADS),
        grid=(b, s // tile),
        in_specs=[xs, ms, ms, _const_spec((1, d)), _const_spec(w_cq.shape), _const_spec(w_co.shape)],
        out_specs=xs,
        out_shape=jax.ShapeDtypeStruct((b, s, d), F32),
        compiler_params=_cparams("arbitrary", "arbitrary"),
        name="cross_attn",
    )(x, mem_k, mem_v, g.reshape(1, d), w_cq, w_co)


def _router_gates(h, wr_split, br):
    h_hi, h_lo = _split_bf16(h)
    parts = _dot(h_hi, wr_split) + _dot(h_lo, wr_split)
    logits = parts + pltpu.roll(parts, LANES // 2, axis=1) + br
    lane = lax.broadcasted_iota(jnp.int32, logits.shape, 1)
    big = jnp.int32(LANES)
    first = lambda mask: jnp.min(jnp.where(mask, lane, big), axis=1, keepdims=True)
    rmax = lambda x: jnp.max(x, axis=1, keepdims=True)

    gmask = (lane >= N_EXPERTS) & (lane < N_EXPERTS + N_GROUPS)
    gl = jnp.where(gmask, logits, NEG_INF)
    gmax = rmax(gl)
    grp = first(gl == gmax) - N_EXPERTS
    p_grp = 1.0 / jnp.sum(jnp.exp(gl - gmax), axis=1, keepdims=True)

    lo = grp * EXPERTS_PER_GROUP
    el = jnp.where((lane >= lo) & (lane < lo + EXPERTS_PER_GROUP), logits, NEG_INF)
    v1 = rmax(el)
    i1 = first(el == v1)
    el2 = jnp.where(lane == i1, NEG_INF, el)
    v2 = rmax(el2)
    i2 = first(el2 == v2)
    t = jnp.exp(v2 - v1)
    w1 = p_grp / (1.0 + t)
    return jnp.where(lane == i1, w1, 0.0) + jnp.where(lane == i2, w1 * t, 0.0), grp


def _split3_bf16(x):
    a = x.astype(BF16)
    r = x - a.astype(F32)
    b = r.astype(BF16)
    return a, b, (r - b.astype(F32)).astype(BF16)


def _moe_kernel(x_ref, g_ref, wr_ref, br_ref, w1_ref, w3_ref, w2_ref, gf_ref, o_ref,
                hs_ref, gs_ref, ys_ref, pt_ref, meta_ref, *, tile, tp, n_sub):
    sg = pl.program_id(1)
    per_step = EXPERTS_PER_GROUP // n_sub

    @pl.when(sg == 0)
    def _route():
        h = _rms(x_ref[...], g_ref[...])
        gates, grp = _router_gates(h, wr_ref[...], br_ref[...])
        lane = lax.broadcasted_iota(jnp.int32, gates.shape, 1)
        member = lane == grp
        r = lax.broadcasted_iota(jnp.int32, (tile, tile), 0)
        c = lax.broadcasted_iota(jnp.int32, (tile, tile), 1)
        tri = jnp.where(r >= c, 1.0, 0.0).astype(BF16)
        csum = _dot(tri, jnp.where(member, 1.0, 0.0).astype(BF16))
        cnt = csum[tile - 1:tile, :]
        padded = jnp.ceil(cnt * (1.0 / BF16_ROWS)) * BF16_ROWS
        lane_row = lane[0:1, :]
        off = jnp.zeros_like(cnt)
        run = jnp.zeros((1, 1), F32)
        for gg in range(N_GROUPS):
            off = jnp.where(lane_row == gg, run, off)
            run = run + padded[:, gg:gg + 1]
        pos = jnp.sum(jnp.where(member, off + csum - 1.0, 0.0), axis=1, keepdims=True)
        pos_row = jnp.transpose(jnp.broadcast_to(pos, (tile, LANES)))[0:1, :]
        pt = jnp.where(lax.broadcasted_iota(jnp.int32, (tile, tp), 1) == pos.astype(jnp.int32), 1.0, 0.0)
        p = jnp.where(lax.broadcasted_iota(jnp.int32, (tp, tile), 0) == pos_row.astype(jnp.int32), 1.0, 0.0)
        p = p.astype(BF16)
        pt_ref[...] = pt.astype(BF16)
        hs_ref[0:tp, :] = _dot(p, h.astype(BF16)).astype(BF16)
        hs_ref[tp:, :] = jnp.zeros((hs_ref.shape[0] - tp, hs_ref.shape[1]), BF16)
        grel = gates
        for gg in range(1, N_GROUPS):
            grel = jnp.where(grp == gg, pltpu.roll(gates, LANES - gg * EXPERTS_PER_GROUP, axis=1), grel)
        g3 = [t.astype(F32) for t in _split3_bf16(grel)]
        packed = g3[0] + pltpu.roll(g3[1], EXPERTS_PER_GROUP, axis=1) + pltpu.roll(g3[2], 2 * EXPERTS_PER_GROUP, axis=1)
        sorted_g = _dot(p, packed.astype(BF16))
        gs_ref[0:tp, :] = (sorted_g + pltpu.roll(sorted_g, LANES - EXPERTS_PER_GROUP, axis=1)
                           + pltpu.roll(sorted_g, LANES - 2 * EXPERTS_PER_GROUP, axis=1))
        gs_ref[tp:, :] = jnp.zeros((gs_ref.shape[0] - tp, gs_ref.shape[1]), F32)
        ys_ref[...] = jnp.zeros(ys_ref.shape, F32)
        for gg in range(N_GROUPS):
            pick = lambda v: jnp.sum(jnp.where(lane_row == gg, v, 0.0)).astype(jnp.int32)
            meta_ref[0, gg] = pick(off)
            meta_ref[1, gg] = pick(cnt)

    grp_id = sg // n_sub
    e0 = (sg % n_sub) * per_step
    start = meta_ref[0, grp_id]
    cnt = meta_ref[1, grp_id]
    lane8 = lax.broadcasted_iota(jnp.int32, (MOE_BLOCK, LANES), 1)

    def block(i, carry):
        r0 = pl.multiple_of(start + i * MOE_BLOCK, BF16_ROWS)
        rows = hs_ref[pl.ds(r0, MOE_BLOCK), :]
        ridx = r0 + lax.broadcasted_iota(jnp.int32, (MOE_BLOCK, 1), 0)
        gts = jnp.where(ridx < start + cnt, gs_ref[pl.ds(r0, MOE_BLOCK), :], 0.0)
        y = None
        for j in range(per_step):
            gate = jnp.sum(jnp.where(lane8 == e0 + j, gts, 0.0), axis=1, keepdims=True)
            h1 = _dot(rows, w1_ref[j])
            h3 = _dot(rows, w3_ref[j])
            act = (h1 * jax.nn.sigmoid(h1)) * h3 * gate
            t = _dot(act.astype(BF16), w2_ref[j])
            y = t if y is None else y + t
        ys_ref[pl.ds(r0, MOE_BLOCK), :] += y
        return carry

    lax.fori_loop(0, (cnt + MOE_BLOCK - 1) // MOE_BLOCK, block, 0)

    @pl.when(sg == pl.num_programs(1) - 1)
    def _finish():
        pt = pt_ref[...]
        y_hi, y_lo = _split_bf16(ys_ref[0:tp, :])
        o_ref[...] = _rms(x_ref[...] + (_dot(pt, y_hi) + _dot(pt, y_lo)), gf_ref[...])


def _moe_final(x, g, wr_split, br, w1, w3, w2, g_final, tile, n_sub):
    n, d = x.shape
    n_e, _, f = w1.shape
    per_step = EXPERTS_PER_GROUP // n_sub
    tp = tile + LANES
    rows = tp + MOE_BLOCK
    row = pl.BlockSpec((tile, d), lambda i, s: (i, 0))
    cst = lambda shape: pl.BlockSpec(shape, lambda i, s: (0,) * len(shape))
    return pl.pallas_call(
        functools.partial(_moe_kernel, tile=tile, tp=tp, n_sub=n_sub),
        grid=(n // tile, N_GROUPS * n_sub),
        in_specs=[row, cst((1, d)), cst(wr_split.shape), cst((1, LANES)),
                  pl.BlockSpec((per_step, d, f), lambda i, s: (s, 0, 0)),
                  pl.BlockSpec((per_step, d, f), lambda i, s: (s, 0, 0)),
                  pl.BlockSpec((per_step, f, d), lambda i, s: (s, 0, 0)),
                  cst((1, d))],
        out_specs=row,
        out_shape=jax.ShapeDtypeStruct((n, d), F32),
        scratch_shapes=[pltpu.VMEM((rows, d), BF16), pltpu.VMEM((rows, LANES), F32), pltpu.VMEM((rows, d), F32),
                        pltpu.VMEM((tile, tp), BF16), pltpu.SMEM((2, N_GROUPS), jnp.int32)],
        compiler_params=_cparams("arbitrary", "arbitrary"),
        name="moe_final",
    )(x, g.reshape(1, d), wr_split, br, w1, w3, w2, g_final.reshape(1, d))


def _t5_bucket(rel):
    n = jnp.maximum(-rel, 0)
    max_exact = NUM_BUCKETS // 2
    nf = jnp.maximum(n, max_exact).astype(F32)
    large = max_exact + (jnp.log(nf / max_exact) / math.log(MAX_DISTANCE / max_exact)
                         * (NUM_BUCKETS - max_exact)).astype(jnp.int32)
    large = jnp.minimum(large, NUM_BUCKETS - 1)
    return jnp.where(n < max_exact, n, large)


def _distance_bias(rel_bias, n_dist, descending=False):
    dist = jnp.arange(n_dist, dtype=jnp.int32)
    bucket = _t5_bucket(-(n_dist - 1 - dist if descending else dist))
    onehot = (bucket[:, None] == jnp.arange(NUM_BUCKETS, dtype=jnp.int32)[None, :]).astype(F32)
    return jnp.dot(onehot, rel_bias.astype(F32), precision=lax.Precision.HIGHEST).T


def _prompt_bias_tiles(rel_bias, tq, nd):
    p = 2 * tq
    table = _distance_bias(rel_bias, nd * tq)
    ext = jnp.pad(table, ((0, 0), (tq, 1)), constant_values=NEG_INF)
    c = jnp.stack([jnp.concatenate([ext[:, d * tq + 1:d * tq + tq + 1][:, ::-1],
                                    ext[:, d * tq + tq + 1:d * tq + p + 1][:, ::-1]], axis=1)
                   for d in range(nd)], axis=1)
    m = jnp.tile(c, (1, 1, tq))[:, :, :tq * (p - 1)].reshape(N_HEADS, nd, tq, p - 1)
    return m[..., :tq]


def _decode_bias(rel_bias, past_len, slots, n_new, page):
    n_dist = past_len + slots
    table = _distance_bias(rel_bias, n_dist, descending=True)
    past = jnp.stack([table[:, slots - 1 - j:slots - 1 - j + past_len] for j in range(slots)], axis=1)
    jq = jnp.arange(slots, dtype=jnp.int32)[:, None]
    jk = jnp.arange(page, dtype=jnp.int32)[None, :]
    visible = (jk <= jq) & (jk < n_new)
    near = table[:, n_dist - slots:][:, ::-1]
    new = jnp.where(visible[None], near[:, jnp.clip(jq - jk, 0, slots - 1)], NEG_INF)
    return past.reshape(N_HEADS * slots, -1), new.reshape(N_HEADS * slots, -1)


def _head_queries(q, n_q, slots):
    nb = q.shape[0] // n_q
    qh = jnp.transpose(q.reshape(nb, n_q, N_HEADS, V_DIM), (0, 2, 1, 3))
    qh = jnp.pad(qh, ((0, 0), (0, 0), (0, slots - n_q), (0, 0)))
    first = jnp.arange(V_DIM) < HEAD_DIM
    zero = jnp.zeros((), q.dtype)
    return jnp.concatenate([jnp.where(first, qh, zero), jnp.where(first, zero, qh)], axis=2)


def _head_pair_queries(q, n_q, slots):
    qh = _head_queries(q, n_q, slots)
    hr = qh.shape[2]
    qt = jnp.swapaxes(qh, 2, 3)
    cols = [jnp.pad(qt[:, h], ((0, 0), (0, 0), (h * hr, (N_HEADS - 1 - h) * hr))) for h in range(N_HEADS)]
    return jnp.stack([jnp.concatenate(cols[2 * pr:2 * pr + 2], axis=1) for pr in range(N_HEADS // 2)], axis=1)


def _head_major(t, n_q, slots):
    nb = t.shape[0] // n_q
    th = jnp.transpose(t.reshape(nb, n_q, N_HEADS, V_DIM), (0, 2, 1, 3))
    return jnp.pad(th, ((0, 0), (0, 0), (0, slots - n_q), (0, 0)))


def _pick_tile(n, pref):
    return pref if n % pref == 0 else n


def kernel(x_prompt, x_sample, mem_prompt, cache_k, cache_v, page_table, cache_mem_k, cache_mem_v, state_pool, rel_bias, norm_mix_g, w_in, lambda_q1, lambda_k1, lambda_q2, lambda_k2, subln_g, w_pool_grp, pool_scale, w_br_attn, w_br_pool, w_gate, w_out, norm_cross_g, w_cq, w_ck, w_cv, w_co, norm_ffn_g, w_router_grp, b_router_grp, w_router_exp, b_router_exp, w_exp_gate, w_exp_up, w_exp_down, norm_final_g):
    depth = w_in.shape[0]
    assert depth == 1, "single-layer step"
    l = 0
    b_p, s_p, d = x_prompt.shape
    b_s, s_s, _ = x_sample.shape
    n_p, n_s = b_p * s_p, b_s * s_s
    n_mem = mem_prompt.shape[1]
    page = cache_k.shape[2]
    past_len = page_table.shape[1] * page
    pool_w = state_pool.shape[-1]

    lam_init = 0.8 - 0.6 * math.exp(-0.3 * l)
    lam = (jnp.exp(jnp.sum(lambda_q1[l].astype(F32) * lambda_k1[l].astype(F32)))
           - jnp.exp(jnp.sum(lambda_q2[l].astype(F32) * lambda_k2[l].astype(F32))) + lam_init).reshape(1, 1)

    bf = lambda w: w.astype(BF16)
    w_in_bf, w_gate_bf, w_ba_bf, w_bp_bf, w_out_bf = bf(w_in[l]), bf(w_gate[l]), bf(w_br_attn[l]), bf(w_br_pool[l]), bf(w_out[l])
    w_cq_bf, w_ck_bf, w_cv_bf, w_co_bf = bf(w_cq[l]), bf(w_ck[l]), bf(w_cv[l]), bf(w_co[l])
    w_grp_bf = bf(w_pool_grp[l])
    w1_bf, w3_bf, w2_bf = bf(w_exp_gate[l]), bf(w_exp_up[l]), bf(w_exp_down[l])
    wr = jnp.concatenate([w_router_exp[l], w_router_grp[l]], axis=1).astype(F32)
    n_router = wr.shape[1]
    assert n_router <= LANES // 2
    wr_hi = wr.astype(BF16)
    wr_lo = (wr - wr_hi.astype(F32)).astype(BF16)
    lane_pad = lambda t, lo: jnp.pad(t, ((0, 0), (lo, LANES - lo - n_router)))
    wr_split = lane_pad(wr_hi, 0) + lane_pad(wr_lo, LANES // 2)
    br = jnp.concatenate([b_router_exp[l], b_router_grp[l]]).astype(F32)
    br = jnp.pad(br, (0, LANES - br.shape[0])).reshape(1, LANES)

    def token_tail(x_tok, a, pooled, mem_k, mem_v, nb, tile, cross_tile):
        x1 = _merge(x_tok, a, pooled, norm_mix_g[l], w_gate_bf, w_ba_bf, w_bp_bf, w_out_bf, tile)
        x1 = x1.reshape(nb, -1, d)
        seq = x1.shape[1]
        x1 = jnp.pad(x1, ((0, 0), (0, -seq % BF16_ROWS), (0, 0)))
        x2 = _cross(x1, mem_k, mem_v, norm_cross_g[l], w_cq_bf, w_co_bf, max(cross_tile, BF16_ROWS))[:, :seq]
        n_tok = x_tok.shape[0]
        return _moe_final(x2.reshape(-1, d), norm_ffn_g[l], wr_split, br, w1_bf, w3_bf, w2_bf,
                          norm_final_g, _pick_tile(n_tok, 1024), 2)

    tile_p = _pick_tile(n_p, 512)
    tq = _pick_tile(s_p, 256)
    q, k, v, u = _inproj(x_prompt.reshape(n_p, d), norm_mix_g[l], w_in_bf, tile_p)
    bias_tiles = _prompt_bias_tiles(rel_bias, tq, N_NEAR + 1)
    a = _prompt_attention(lam, q.reshape(b_p, s_p, d), k.reshape(b_p, s_p, d), v.reshape(b_p, s_p, d),
                          bias_tiles, subln_g[l], lam_init, tq)
    u3 = u.reshape(b_p, s_p, pool_w)
    pooled = _pool_mix(u3, w_grp_bf, pool_scale[l], _pick_tile(s_p, 512))
    mem_k, mem_v = _memkv(mem_prompt.reshape(b_p * n_mem, d), w_ck_bf, w_cv_bf, _pick_tile(b_p * n_mem, 512))
    mem_k, mem_v = mem_k.reshape(b_p, n_mem, d), mem_v.reshape(b_p, n_mem, d)
    y_prompt = token_tail(x_prompt.reshape(n_p, d), a.reshape(n_p, d), pooled.reshape(n_p, pool_w),
                          mem_k, mem_v, b_p, tile_p, _pick_tile(s_p, 512)).reshape(b_p, s_p, d)
    k_prompt = k.reshape(1, b_p, s_p, N_HEADS, V_DIM)
    v_prompt = v.reshape(1, b_p, s_p, N_HEADS, V_DIM)
    pool_prompt = u3[:, -POOL_CTX:][None]
    x_heads_dim = d // X_HEADS
    mem_k_prompt = mem_k.reshape(1, b_p, n_mem, X_HEADS, x_heads_dim)
    mem_v_prompt = mem_v.reshape(1, b_p, n_mem, X_HEADS, x_heads_dim)

    qs, ks, vs, us = _inproj(x_sample.reshape(n_s, d), norm_mix_g[l], w_in_bf, n_s)
    slots = -(-s_s // 8) * 8
    bias_past, bias_new = _decode_bias(rel_bias, past_len, slots, s_s, page)
    paged = lambda c: c.reshape(-1, page * N_HEADS, V_DIM)
    a_s = _decode_attention(page_table.astype(jnp.int32), lam, _head_pair_queries(qs, s_s, slots),
                            paged(cache_k), paged(cache_v), bias_past,
                            _head_major(ks, s_s, slots), _head_major(vs, s_s, slots), bias_new,
                            subln_g[l], lam_init, page)[:, :s_s]
    ctx = state_pool[l].astype(F32)
    full = jnp.concatenate([ctx, us.reshape(b_s, s_s, pool_w)], axis=1)
    slot = full.shape[1] + (-full.shape[1] % (2 * BF16_ROWS))
    stream = jnp.pad(full, ((0, 0), (0, slot - full.shape[1]), (0, 0))).reshape(1, b_s * slot, pool_w)
    pooled_s = _pool_mix(stream, w_grp_bf, pool_scale[l], _pick_tile(b_s * slot, 256))
    pooled_s = pooled_s.reshape(b_s, slot, pool_w)[:, POOL_CTX:POOL_CTX + s_s]
    y_sample = token_tail(x_sample.reshape(n_s, d), a_s.reshape(n_s, d).astype(BF16), pooled_s.reshape(n_s, pool_w),
                          cache_mem_k[l].reshape(b_s, n_mem, d), cache_mem_v[l].reshape(b_s, n_mem, d),
                          b_s, n_s, s_s).reshape(b_s, s_s, d)
    k_sample = ks.reshape(1, b_s, s_s, N_HEADS, V_DIM)
    v_sample = vs.reshape(1, b_s, s_s, N_HEADS, V_DIM)
    pool_sample = full[:, -POOL_CTX:][None]

    return (y_prompt, y_sample, k_prompt, v_prompt, pool_prompt, mem_k_prompt, mem_v_prompt,
            k_sample, v_sample, pool_sample)
```

```python
import functools
import math

import jax
import jax.numpy as jnp
from jax import lax
from jax.experimental import pallas as pl
from jax.experimental.pallas import tpu as pltpu

F32 = jnp.float32
BF16 = jnp.bfloat16

RMS_EPS = 1e-6
N_HEADS = 8
HEAD_DIM = 64
V_DIM = 2 * HEAD_DIM
NUM_BUCKETS = 32
MAX_DISTANCE = 128
POOL_WINDOWS = (2, 4, 8, 16)
POOL_GROUP_DIM = 128
POOL_CTX = max(POOL_WINDOWS) - 1
X_HEADS = 4
N_GROUPS = 4
EXPERTS_PER_GROUP = 8
N_EXPERTS = N_GROUPS * EXPERTS_PER_GROUP
LANES = 128
BF16_ROWS = 16
VMEM_LIMIT = 56 * 1024 * 1024
PAGES_PER_STEP = 16
CROSS_SEQS_PER_STEP = 4
MOE_BLOCK = 288
NEG_INF = float("-inf")


def _cparams(*sem):
    return pltpu.CompilerParams(dimension_semantics=sem, vmem_limit_bytes=VMEM_LIMIT)


def _rms(x, g):
    return x * lax.rsqrt(jnp.mean(x * x, axis=-1, keepdims=True) + RMS_EPS) * g


def _dot(a, b):
    return jnp.dot(a, b, preferred_element_type=F32)


def _dot_nt(a, b):
    return lax.dot_general(a, b, (((1,), (1,)), ((), ())), preferred_element_type=F32)


def _const_spec(shape):
    zeros = (0,) * len(shape)
    return pl.BlockSpec(shape, lambda *_: zeros)


POOL_HALO = 16


def _split_bf16(x):
    hi = x.astype(BF16)
    return hi, (x - hi.astype(F32)).astype(BF16)


def _pool_rows(u, prev, pos0, w_ref, scale_ref, o_ref):
    tile = u.shape[0]
    pos = pos0 + lax.broadcasted_iota(jnp.int32, (tile, 1), 0)
    for g, w in enumerate(POOL_WINDOWS):
        sl = slice(g * POOL_GROUP_DIM, (g + 1) * POOL_GROUP_DIM)
        ug = u[:, sl]
        tot = jnp.concatenate([prev[:, sl], ug], axis=0)
        span = 1
        while span < w:
            tot = tot + pltpu.roll(tot, span, axis=0)
            span *= 2
        cnt = jnp.minimum(pos + 1, w).astype(F32)
        pooled = tot[POOL_HALO:] / cnt - ug
        mixed = _dot(pooled.astype(BF16), w_ref[g]) * scale_ref[:, sl]
        o_ref[:, sl] = mixed.astype(o_ref.dtype)


def _inproj_kernel(x_ref, g_ref, w_ref, *rest, d, tiles_per_seq):
    if tiles_per_seq:
        wg_ref, sc_ref, q_ref, k_ref, v_ref, u_ref, p_ref, halo_ref = rest
    else:
        q_ref, k_ref, v_ref, u_ref = rest
    h = _rms(x_ref[...], g_ref[...]).astype(BF16)
    q_ref[...] = (_dot(h, w_ref[:, 0:d]) * (HEAD_DIM ** -0.5)).astype(BF16)
    k_ref[...] = _dot(h, w_ref[:, d:2 * d])
    v_ref[...] = _dot(h, w_ref[:, 2 * d:3 * d])
    u = _dot(h, w_ref[:, 3 * d:])
    u_ref[...] = u
    if tiles_per_seq:
        tile = u.shape[0]
        i = pl.program_id(0) % tiles_per_seq

        @pl.when(i == 0)
        def _():
            halo_ref[...] = jnp.zeros(halo_ref.shape, F32)

        _pool_rows(u, halo_ref[...], i * tile, wg_ref, sc_ref, p_ref)
        halo_ref[...] = u[tile - POOL_HALO:, :]


def _inproj(x, g, w_bf, tile, pool=None):
    n, d = x.shape
    wu = w_bf.shape[1] - 3 * d
    row = lambda w: pl.BlockSpec((tile, w), lambda i: (i, 0))
    in_specs = [row(d), _const_spec((1, d)), _const_spec(w_bf.shape)]
    out_specs = [row(d), row(d), row(d), row(wu)]
    out_shape = [jax.ShapeDtypeStruct((n, d), BF16), jax.ShapeDtypeStruct((n, d), F32),
                 jax.ShapeDtypeStruct((n, d), F32), jax.ShapeDtypeStruct((n, wu), F32)]
    args, scratch, tiles_per_seq = [x, g.reshape(1, d), w_bf], [], 0
    if pool is not None:
        tiles_per_seq, w_grp_bf, scale = pool
        in_specs += [_const_spec(w_grp_bf.shape), _const_spec((1, wu))]
        out_specs.append(row(wu))
        out_shape.append(jax.ShapeDtypeStruct((n, wu), BF16))
        args += [w_grp_bf, scale.reshape(1, wu)]
        scratch = [pltpu.VMEM((POOL_HALO, wu), F32)]
    return pl.pallas_call(
        functools.partial(_inproj_kernel, d=d, tiles_per_seq=tiles_per_seq),
        grid=(n // tile,),
        in_specs=in_specs,
        out_specs=out_specs,
        out_shape=out_shape,
        scratch_shapes=scratch,
        compiler_params=_cparams("arbitrary"),
        name="inproj",
    )(*args)


def _head_norm(o, g, lam_init):
    return _rms(o, g) * (1.0 - lam_init)


def _lane_chunks(x):
    return [x[:, c * LANES:(c + 1) * LANES] for c in range(x.shape[1] // LANES)]


N_NEAR = 2


def _pattn_kernel(lam_ref, q_ref, k_ref, v_ref, bias_ref, g_ref, o_ref, qz_ref, kb_ref, vb_ref,
                  mxn_ref, mxf_ref, mbn_ref, mbf_ref, ls_ref, acc_ref, *s_refs, tq, nq, lam_init):
    s_len = q_ref.shape[0]
    near = N_NEAR * tq
    kb_ref[...] = k_ref[...].astype(BF16)
    vb_ref[...] = v_ref[...].astype(BF16)
    q = q_ref[...]
    lane = lax.broadcasted_iota(jnp.int32, q.shape, 1)
    zero = jnp.zeros_like(q)
    qz_ref[0] = jnp.where(lane < HEAD_DIM, q, zero)
    qz_ref[1] = jnp.where(lane >= HEAD_DIM, q, zero)
    c_far = bias_ref[N_NEAR][0:1, 0:1]
    mxn_ref[...] = jnp.full(mxn_ref.shape, NEG_INF, F32)
    mxf_ref[...] = jnp.full(mxf_ref.shape, NEG_INF, F32)

    def row_parts(kb):
        r0 = kb * tq
        parts = [(r0 + d * tq, r0 + (d + 1) * tq, True) for d in range(N_NEAR) if r0 + d * tq < s_len]
        if r0 + near < s_len:
            parts.append((r0 + near, s_len, False))
        return parts

    for kb in range(nq):
        r0 = kb * tq
        keys = kb_ref[r0:r0 + tq, :]
        for m in range(2):
            s = _dot_nt(qz_ref[m, r0:, :], keys)
            for lo, hi, is_near in row_parts(kb):
                t = s[lo - r0:hi - r0]
                if is_near:
                    t = t + bias_ref[(lo - r0) // tq]
                mx_ref = mxn_ref if is_near else mxf_ref
                mx_ref[m, lo:hi] = functools.reduce(jnp.maximum, _lane_chunks(t), mx_ref[m, lo:hi])
                s_refs[kb][m, lo - r0:hi - r0] = t

    mrow = jnp.maximum(jnp.max(mxn_ref[...], axis=2, keepdims=True),
                       jnp.max(mxf_ref[...], axis=2, keepdims=True) + c_far)
    mbn_ref[...] = jnp.broadcast_to(mrow, mbn_ref.shape)
    mbf_ref[...] = jnp.broadcast_to(mrow - c_far, mbf_ref.shape)
    ls_ref[...] = jnp.zeros(ls_ref.shape, F32)
    acc_ref[...] = jnp.zeros(acc_ref.shape, F32)

    for kb in range(nq):
        r0 = kb * tq
        vals = vb_ref[r0:r0 + tq, :]
        for m in range(2):
            for lo, hi, is_near in row_parts(kb):
                mb = (mbn_ref if is_near else mbf_ref)[m, lo:hi]
                p = [jnp.exp(c - mb) for c in _lane_chunks(s_refs[kb][m, lo - r0:hi - r0])]
                ls_ref[m, lo:hi] += functools.reduce(jnp.add, p)
                acc_ref[m, lo:hi] += _dot(jnp.concatenate(p, axis=1).astype(BF16), vals)

    o = acc_ref[...] / jnp.sum(ls_ref[...], axis=2, keepdims=True)
    o = o[0] - lam_ref[0, 0] * o[1]
    o_ref[...] = _head_norm(o, g_ref[...], lam_init).astype(o_ref.dtype)


def _prompt_attention(lam, q, k, v, bias_tiles, subln_g, lam_init, tq):
    b, s, d = q.shape
    nq = s // tq
    assert tq + 1 >= MAX_DISTANCE and bias_tiles.shape[1] == N_NEAR + 1
    seq = pl.BlockSpec((None, s, V_DIM), lambda h, bi: (bi, 0, h))
    stat = pltpu.VMEM((2, s, LANES), F32)
    return pl.pallas_call(
        functools.partial(_pattn_kernel, tq=tq, nq=nq, lam_init=lam_init),
        grid=(N_HEADS, b),
        in_specs=[pl.BlockSpec(memory_space=pltpu.SMEM), seq, seq, seq,
                  pl.BlockSpec((None, N_NEAR + 1, tq, tq), lambda h, bi: (h, 0, 0, 0)),
                  _const_spec((1, V_DIM))],
        out_specs=seq,
        out_shape=jax.ShapeDtypeStruct((b, s, d), BF16),
        scratch_shapes=[pltpu.VMEM((2, s, V_DIM), BF16), pltpu.VMEM((s, V_DIM), BF16), pltpu.VMEM((s, V_DIM), BF16),
                        stat, stat, stat, stat, stat, pltpu.VMEM((2, s, V_DIM), F32)]
                       + [pltpu.VMEM((2, s - kb * tq, tq), F32) for kb in range(nq)],
        compiler_params=_cparams("arbitrary", "arbitrary"),
        name="prompt_attn",
    )(lam, q, k, v, bias_tiles, subln_g.reshape(1, V_DIM))


def _dattn_kernel(pt_ref, lam_ref, q_ref, *refs, page, lam_init):
    del pt_ref
    npg = PAGES_PER_STEP
    k_refs, v_refs = refs[:npg], refs[npg:2 * npg]
    bias_ref, kn_ref, vn_ref, biasn_ref, g_ref, o_ref, m_ref, l_ref, acc_ref = refs[2 * npg:]
    g_idx, b = pl.program_id(0), pl.program_id(1)
    hr = m_ref.shape[1] // N_HEADS
    half = hr // 2

    @pl.when(g_idx == 0)
    def _():
        m_ref[b] = jnp.full(m_ref.shape[1:], NEG_INF, F32)
        l_ref[b] = jnp.zeros(l_ref.shape[1:], F32)
        acc_ref[b] = jnp.zeros(acc_ref.shape[1:], F32)

    def head_rows(ref, h):
        return ref[pl.ds(h, page, stride=N_HEADS), :].astype(BF16)

    def update(s, v_blocks):
        m_old = m_ref[b]
        m_new = jnp.maximum(m_old, jnp.max(s, axis=1, keepdims=True))
        alpha = jnp.exp(m_old - m_new)
        p = jnp.exp(s - m_new)
        l_ref[b] = alpha * l_ref[b] + jnp.sum(p, axis=1, keepdims=True)
        m_ref[b] = m_new
        pb = p.astype(BF16)
        for h in range(N_HEADS):
            rs = slice(h * hr, (h + 1) * hr)
            pv = None
            for i, v in enumerate(v_blocks[h]):
                t = _dot(pb[rs, i * page:(i + 1) * page], v)
                pv = t if pv is None else pv + t
            acc_ref[b, h] = alpha[rs] * acc_ref[b, h] + pv

    def page_scores(k_of_head):
        acc = None
        for pr in range(N_HEADS // 2):
            keys = jnp.concatenate([k_of_head(2 * pr), k_of_head(2 * pr + 1)], axis=1)
            t = _dot(keys, q_ref[pr])
            acc = t if acc is None else acc + t
        return acc.T

    def both_softmaxes(bias):
        return jnp.concatenate([bias[h * half:(h + 1) * half] for h in range(N_HEADS) for _ in range(2)], axis=0)

    s = jnp.concatenate([page_scores(lambda h, kr=kr: head_rows(kr, h)) for kr in k_refs], axis=1)
    update(s + both_softmaxes(bias_ref[...]), [[head_rows(vr, h) for vr in v_refs] for h in range(N_HEADS)])

    @pl.when(g_idx == pl.num_programs(0) - 1)
    def _():
        def pad_page(x):
            return jnp.concatenate([x, jnp.zeros((page - x.shape[0], x.shape[1]), x.dtype)], axis=0).astype(BF16)

        sn = page_scores(lambda h: pad_page(kn_ref[h]))
        update(sn + both_softmaxes(biasn_ref[...]), [[pad_page(vn_ref[h])] for h in range(N_HEADS)])
        linv = 1.0 / l_ref[b]
        g = g_ref[...]
        for h in range(N_HEADS):
            o = acc_ref[b, h] * linv[h * hr:(h + 1) * hr]
            o = o[:half] - lam_ref[0, 0] * o[half:]
            o_ref[b, :, h * V_DIM:(h + 1) * V_DIM] = _head_norm(o, g, lam_init)


def _decode_attention(page_table, lam, q_heads, cache_k, cache_v, bias_past, k_new, v_new, bias_new,
                      subln_g, lam_init, page):
    nb, n_pages = page_table.shape
    npg = PAGES_PER_STEP
    rows = q_heads.shape[-1]
    hr = rows // N_HEADS
    slots = hr // 2
    d = N_HEADS * V_DIM
    assert rows == LANES

    def page_spec(i):
        return pl.BlockSpec((None, page * N_HEADS, V_DIM), lambda g, b, pt: (pt[b, g * npg + i], 0, 0))

    seq = lambda *blk: pl.BlockSpec((None,) + blk, lambda g, b, pt: (b,) + (0,) * len(blk))
    in_specs = ([pl.BlockSpec(memory_space=pltpu.SMEM), seq(N_HEADS // 2, 2 * V_DIM, rows)]
                + [page_spec(i) for i in range(npg)] * 2
                + [pl.BlockSpec((rows // 2, npg * page), lambda g, b, pt: (0, g)),
                   seq(N_HEADS, slots, V_DIM), seq(N_HEADS, slots, V_DIM),
                   pl.BlockSpec((rows // 2, page), lambda g, b, pt: (0, 0)),
                   pl.BlockSpec((1, V_DIM), lambda g, b, pt: (0, 0))])
    return pl.pallas_call(
        functools.partial(_dattn_kernel, page=page, lam_init=lam_init),
        grid_spec=pltpu.PrefetchScalarGridSpec(
            num_scalar_prefetch=1,
            grid=(n_pages // npg, nb),
            in_specs=in_specs,
            out_specs=pl.BlockSpec((nb, slots, d), lambda g, b, pt: (0, 0, 0)),
            scratch_shapes=[pltpu.VMEM((nb, rows, 1), F32), pltpu.VMEM((nb, rows, 1), F32),
                            pltpu.VMEM((nb, N_HEADS, hr, V_DIM), F32)]),
        out_shape=jax.ShapeDtypeStruct((nb, slots, d), F32),
        compiler_params=_cparams("arbitrary", "arbitrary"),
        name="decode_attn",
    )(page_table, lam, q_heads, *([cache_k] * npg), *([cache_v] * npg), bias_past, k_new, v_new, bias_new,
      subln_g.reshape(1, V_DIM))


def _pool_kernel(u_ref, prev_ref, w_ref, scale_ref, o_ref, *, tile):
    i = pl.program_id(1)
    prev = jnp.where(i > 0, prev_ref[...], 0.0)
    _pool_rows(u_ref[...], prev, i * tile, w_ref, scale_ref, o_ref)


def _pool_mix(u, w_grp_bf, scale, tile):
    b, s, wdt = u.shape
    assert max(POOL_WINDOWS) <= POOL_HALO and tile % POOL_HALO == 0
    per_halo = tile // POOL_HALO
    return pl.pallas_call(
        functools.partial(_pool_kernel, tile=tile),
        grid=(b, s // tile),
        in_specs=[pl.BlockSpec((None, tile, wdt), lambda bi, i: (bi, i, 0)),
                  pl.BlockSpec((None, POOL_HALO, wdt), lambda bi, i: (bi, jnp.maximum(i * per_halo - 1, 0), 0)),
                  _const_spec(w_grp_bf.shape), _const_spec((1, wdt))],
        out_specs=pl.BlockSpec((None, tile, wdt), lambda bi, i: (bi, i, 0)),
        out_shape=jax.ShapeDtypeStruct((b, s, wdt), BF16),
        compiler_params=_cparams("arbitrary", "arbitrary"),
        name="pool_mix",
    )(u, u, w_grp_bf, scale.reshape(1, wdt))


def _merge_kernel(x_ref, a_ref, p_ref, g_ref, wg_ref, wa_ref, wp_ref, wo_ref, o_ref, *, d):
    x = x_ref[...]
    h = _rms(x, g_ref[...]).astype(BF16)
    ga = jax.nn.sigmoid(_dot(h, wg_ref[:, :d]))
    merged = ga * _dot(a_ref[...], wa_ref[...])
    gp = jax.nn.sigmoid(_dot(h, wg_ref[:, d:]))
    merged += gp * _dot(p_ref[...], wp_ref[...])
    o_ref[...] = x + _dot(merged.astype(BF16), wo_ref[...])


def _merge(x, a, pooled, g, w_gate, w_br_attn, w_br_pool, w_out, tile):
    n, d = x.shape
    row = lambda w: pl.BlockSpec((tile, w), lambda i: (i, 0))
    return pl.pallas_call(
        functools.partial(_merge_kernel, d=d),
        grid=(n // tile,),
        in_specs=[row(d), row(a.shape[1]), row(pooled.shape[1]), _const_spec((1, d)),
                  _const_spec(w_gate.shape), _const_spec(w_br_attn.shape), _const_spec(w_br_pool.shape),
                  _const_spec(w_out.shape)],
        out_specs=row(d),
        out_shape=jax.ShapeDtypeStruct((n, d), F32),
        compiler_params=_cparams("arbitrary"),
        name="merge",
    )(x, a, pooled, g.reshape(1, d), w_gate, w_br_attn, w_br_pool, w_out)


def _memkv_kernel(x_ref, wk_ref, wv_ref, k_ref, v_ref):
    x = x_ref[...].astype(BF16)
    k_ref[...] = _dot(x, wk_ref[...])
    v_ref[...] = _dot(x, wv_ref[...])


def _memkv(x, wk, wv, tile):
    n, d = x.shape
    row = pl.BlockSpec((tile, d), lambda i: (i, 0))
    return pl.pallas_call(
        _memkv_kernel,
        grid=(n // tile,),
        in_specs=[row, _const_spec(wk.shape), _const_spec(wv.shape)],
        out_specs=[row, row],
        out_shape=[jax.ShapeDtypeStruct((n, d), F32)] * 2,
        compiler_params=_cparams("arbitrary"),
        name="mem_kv",
    )(x, wk, wv)


def _cross_kernel(x_ref, mk_ref, mv_ref, g_ref, wq_ref, wo_ref, o_ref, *, dh):
    n_seq, tile, d = x_ref.shape
    x = x_ref[...].reshape(n_seq * tile, d)
    h = _rms(x, g_ref[...]).astype(BF16)
    q = (_dot(h, wq_ref[...]) * (dh ** -0.5)).astype(BF16)
    out = x
    for hh in range(X_HEADS):
        sl = slice(hh * dh, (hh + 1) * dh)
        heads = []
        for i in range(n_seq):
            s = _dot_nt(q[i * tile:(i + 1) * tile, sl], mk_ref[i, :, sl].astype(BF16))
            p = jnp.exp(s - jnp.max(s, axis=1, keepdims=True))
            heads.append(_dot(p.astype(BF16), mv_ref[i, :, sl].astype(BF16)) / jnp.sum(p, axis=1, keepdims=True))
        o = heads[0] if n_seq == 1 else jnp.concatenate(heads, axis=0)
        out += _dot(o.astype(BF16), wo_ref[sl, :])
    o_ref[...] = out.reshape(n_seq, tile, d)


def _cross(x, mem_k, mem_v, g, w_cq, w_co, tile, n_seq=1):
    b, s, d = x.shape
    m = mem_k.shape[1]
    xs = pl.BlockSpec((n_seq, tile, d), lambda bi, i: (bi, i, 0))
    ms = pl.BlockSpec((n_seq, m, d), lambda bi, i: (bi, 0, 0))
    return pl.pallas_call(
        functools.partial(_cross_kernel, dh=d // X_HEADS),
        grid=(b // n_seq, s // tile),
        in_specs=[xs, ms, ms, _const_spec((1, d)), _const_spec(w_cq.shape), _const_spec(w_co.shape)],
        out_specs=xs,
        out_shape=jax.ShapeDtypeStruct((b, s, d), F32),
        compiler_params=_cparams("arbitrary", "arbitrary"),
        name="cross_attn",
    )(x, mem_k, mem_v, g.reshape(1, d), w_cq, w_co)


def _router_gates(h, wr_split, br):
    h_hi, h_lo = _split_bf16(h)
    parts = _dot(h_hi, wr_split) + _dot(h_lo, wr_split)
    logits = parts + pltpu.roll(parts, LANES // 2, axis=1) + br
    lane = lax.broadcasted_iota(jnp.int32, logits.shape, 1)
    big = jnp.int32(LANES)
    first = lambda mask: jnp.min(jnp.where(mask, lane, big), axis=1, keepdims=True)
    rmax = lambda x: jnp.max(x, axis=1, keepdims=True)

    gmask = (lane >= N_EXPERTS) & (lane < N_EXPERTS + N_GROUPS)
    gl = jnp.where(gmask, logits, NEG_INF)
    gmax = rmax(gl)
    grp = first(gl == gmax) - N_EXPERTS
    p_grp = 1.0 / jnp.sum(jnp.exp(gl - gmax), axis=1, keepdims=True)

    lo = grp * EXPERTS_PER_GROUP
    el = jnp.where((lane >= lo) & (lane < lo + EXPERTS_PER_GROUP), logits, NEG_INF)
    v1 = rmax(el)
    i1 = first(el == v1)
    el2 = jnp.where(lane == i1, NEG_INF, el)
    v2 = rmax(el2)
    i2 = first(el2 == v2)
    t = jnp.exp(v2 - v1)
    w1 = p_grp / (1.0 + t)
    return jnp.where(lane == i1, w1, 0.0) + jnp.where(lane == i2, w1 * t, 0.0), grp


def _split3_bf16(x):
    a = x.astype(BF16)
    r = x - a.astype(F32)
    b = r.astype(BF16)
    return a, b, (r - b.astype(F32)).astype(BF16)


def _moe_kernel(x_ref, g_ref, wr_ref, br_ref, w1_ref, w3_ref, w2_ref, gf_ref, o_ref,
                hs_ref, gs_ref, ys_ref, pt_ref, meta_ref, *, tile, tp, n_sub):
    sg = pl.program_id(1)
    per_step = EXPERTS_PER_GROUP // n_sub

    @pl.when(sg == 0)
    def _route():
        h = _rms(x_ref[...], g_ref[...])
        gates, grp = _router_gates(h, wr_ref[...], br_ref[...])
        lane = lax.broadcasted_iota(jnp.int32, gates.shape, 1)
        member = lane == grp
        r = lax.broadcasted_iota(jnp.int32, (tile, tile), 0)
        c = lax.broadcasted_iota(jnp.int32, (tile, tile), 1)
        tri = jnp.where(r >= c, 1.0, 0.0).astype(BF16)
        csum = _dot(tri, jnp.where(member, 1.0, 0.0).astype(BF16))
        cnt = csum[tile - 1:tile, :]
        padded = jnp.ceil(cnt * (1.0 / BF16_ROWS)) * BF16_ROWS
        lane_row = lane[0:1, :]
        off = jnp.zeros_like(cnt)
        run = jnp.zeros((1, 1), F32)
        for gg in range(N_GROUPS):
            off = jnp.where(lane_row == gg, run, off)
            run = run + padded[:, gg:gg + 1]
        pos = jnp.sum(jnp.where(member, off + csum - 1.0, 0.0), axis=1, keepdims=True)
        pos_row = jnp.transpose(jnp.broadcast_to(pos, (tile, LANES)))[0:1, :]
        pt = jnp.where(lax.broadcasted_iota(jnp.int32, (tile, tp), 1) == pos.astype(jnp.int32), 1.0, 0.0)
        p = jnp.where(lax.broadcasted_iota(jnp.int32, (tp, tile), 0) == pos_row.astype(jnp.int32), 1.0, 0.0)
        p = p.astype(BF16)
        pt_ref[...] = pt.astype(BF16)
        hs_ref[0:tp, :] = _dot(p, h.astype(BF16)).astype(BF16)
        hs_ref[tp:, :] = jnp.zeros((hs_ref.shape[0] - tp, hs_ref.shape[1]), BF16)
        grel = gates
        for gg in range(1, N_GROUPS):
            grel = jnp.where(grp == gg, pltpu.roll(gates, LANES - gg * EXPERTS_PER_GROUP, axis=1), grel)
        g3 = [t.astype(F32) for t in _split3_bf16(grel)]
        packed = g3[0] + pltpu.roll(g3[1], EXPERTS_PER_GROUP, axis=1) + pltpu.roll(g3[2], 2 * EXPERTS_PER_GROUP, axis=1)
        sorted_g = _dot(p, packed.astype(BF16))
        gs_ref[0:tp, :] = (sorted_g + pltpu.roll(sorted_g, LANES - EXPERTS_PER_GROUP, axis=1)
                           + pltpu.roll(sorted_g, LANES - 2 * EXPERTS_PER_GROUP, axis=1))
        gs_ref[tp:, :] = jnp.zeros((gs_ref.shape[0] - tp, gs_ref.shape[1]), F32)
        ys_ref[...] = jnp.zeros(ys_ref.shape, F32)
        for gg in range(N_GROUPS):
            pick = lambda v: jnp.sum(jnp.where(lane_row == gg, v, 0.0)).astype(jnp.int32)
            meta_ref[0, gg] = pick(off)
            meta_ref[1, gg] = pick(cnt)

    grp_id = sg // n_sub
    e0 = (sg % n_sub) * per_step
    start = meta_ref[0, grp_id]
    cnt = meta_ref[1, grp_id]
    lane8 = lax.broadcasted_iota(jnp.int32, (MOE_BLOCK, LANES), 1)

    def block(i, carry):
        r0 = pl.multiple_of(start + i * MOE_BLOCK, BF16_ROWS)
        rows = hs_ref[pl.ds(r0, MOE_BLOCK), :]
        ridx = r0 + lax.broadcasted_iota(jnp.int32, (MOE_BLOCK, 1), 0)
        gts = jnp.where(ridx < start + cnt, gs_ref[pl.ds(r0, MOE_BLOCK), :], 0.0)
        y = None
        for j in range(per_step):
            gate = jnp.sum(jnp.where(lane8 == e0 + j, gts, 0.0), axis=1, keepdims=True)
            h1 = _dot(rows, w1_ref[j])
            h3 = _dot(rows, w3_ref[j])
            act = (h1 * jax.nn.sigmoid(h1)) * h3 * gate
            t = _dot(act.astype(BF16), w2_ref[j])
            y = t if y is None else y + t
        ys_ref[pl.ds(r0, MOE_BLOCK), :] += y
        return carry

    lax.fori_loop(0, (cnt + MOE_BLOCK - 1) // MOE_BLOCK, block, 0)

    @pl.when(sg == pl.num_programs(1) - 1)
    def _finish():
        pt = pt_ref[...]
        y_hi, y_lo = _split_bf16(ys_ref[0:tp, :])
        o_ref[...] = _rms(x_ref[...] + (_dot(pt, y_hi) + _dot(pt, y_lo)), gf_ref[...])


def _moe_final(x, g, wr_split, br, w1, w3, w2, g_final, tile, n_sub):
    n, d = x.shape
    n_e, _, f = w1.shape
    per_step = EXPERTS_PER_GROUP // n_sub
    tp = tile + LANES
    rows = tp + MOE_BLOCK
    row = pl.BlockSpec((tile, d), lambda i, s: (i, 0))
    cst = lambda shape: pl.BlockSpec(shape, lambda i, s: (0,) * len(shape))
    return pl.pallas_call(
        functools.partial(_moe_kernel, tile=tile, tp=tp, n_sub=n_sub),
        grid=(n // tile, N_GROUPS * n_sub),
        in_specs=[row, cst((1, d)), cst(wr_split.shape), cst((1, LANES)),
                  pl.BlockSpec((per_step, d, f), lambda i, s: (s, 0, 0)),
                  pl.BlockSpec((per_step, d, f), lambda i, s: (s, 0, 0)),
                  pl.BlockSpec((per_step, f, d), lambda i, s: (s, 0, 0)),
                  cst((1, d))],
        out_specs=row,
        out_shape=jax.ShapeDtypeStruct((n, d), F32),
        scratch_shapes=[pltpu.VMEM((rows, d), BF16), pltpu.VMEM((rows, LANES), F32), pltpu.VMEM((rows, d), F32),
                        pltpu.VMEM((tile, tp), BF16), pltpu.SMEM((2, N_GROUPS), jnp.int32)],
        compiler_params=_cparams("arbitrary", "arbitrary"),
        name="moe_final",
    )(x, g.reshape(1, d), wr_split, br, w1, w3, w2, g_final.reshape(1, d))


def _t5_bucket(rel):
    n = jnp.maximum(-rel, 0)
    max_exact = NUM_BUCKETS // 2
    nf = jnp.maximum(n, max_exact).astype(F32)
    large = max_exact + (jnp.log(nf / max_exact) / math.log(MAX_DISTANCE / max_exact)
                         * (NUM_BUCKETS - max_exact)).astype(jnp.int32)
    large = jnp.minimum(large, NUM_BUCKETS - 1)
    return jnp.where(n < max_exact, n, large)


def _distance_bias(rel_bias, n_dist, descending=False):
    dist = jnp.arange(n_dist, dtype=jnp.int32)
    bucket = _t5_bucket(-(n_dist - 1 - dist if descending else dist))
    onehot = (bucket[:, None] == jnp.arange(NUM_BUCKETS, dtype=jnp.int32)[None, :]).astype(F32)
    return jnp.dot(onehot, rel_bias.astype(F32), precision=lax.Precision.HIGHEST).T


def _prompt_bias_tiles(rel_bias, tq, nd):
    p = 2 * tq
    table = _distance_bias(rel_bias, nd * tq)
    ext = jnp.pad(table, ((0, 0), (tq, 1)), constant_values=NEG_INF)
    c = jnp.stack([jnp.concatenate([ext[:, d * tq + 1:d * tq + tq + 1][:, ::-1],
                                    ext[:, d * tq + tq + 1:d * tq + p + 1][:, ::-1]], axis=1)
                   for d in range(nd)], axis=1)
    m = jnp.tile(c, (1, 1, tq))[:, :, :tq * (p - 1)].reshape(N_HEADS, nd, tq, p - 1)
    return m[..., :tq]


def _decode_bias(rel_bias, past_len, slots, n_new, page):
    n_dist = past_len + slots
    table = _distance_bias(rel_bias, n_dist, descending=True)
    past = jnp.stack([table[:, slots - 1 - j:slots - 1 - j + past_len] for j in range(slots)], axis=1)
    jq = jnp.arange(slots, dtype=jnp.int32)[:, None]
    jk = jnp.arange(page, dtype=jnp.int32)[None, :]
    visible = (jk <= jq) & (jk < n_new)
    near = table[:, n_dist - slots:][:, ::-1]
    new = jnp.where(visible[None], near[:, jnp.clip(jq - jk, 0, slots - 1)], NEG_INF)
    return past.reshape(N_HEADS * slots, -1), new.reshape(N_HEADS * slots, -1)


def _head_queries(q, n_q, slots):
    nb = q.shape[0] // n_q
    qh = jnp.transpose(q.reshape(nb, n_q, N_HEADS, V_DIM), (0, 2, 1, 3))
    qh = jnp.pad(qh, ((0, 0), (0, 0), (0, slots - n_q), (0, 0)))
    first = jnp.arange(V_DIM) < HEAD_DIM
    zero = jnp.zeros((), q.dtype)
    return jnp.concatenate([jnp.where(first, qh, zero), jnp.where(first, zero, qh)], axis=2)


def _head_pair_queries(q, n_q, slots):
    qh = _head_queries(q, n_q, slots)
    hr = qh.shape[2]
    qt = jnp.swapaxes(qh, 2, 3)
    cols = [jnp.pad(qt[:, h], ((0, 0), (0, 0), (h * hr, (N_HEADS - 1 - h) * hr))) for h in range(N_HEADS)]
    return jnp.stack([jnp.concatenate(cols[2 * pr:2 * pr + 2], axis=1) for pr in range(N_HEADS // 2)], axis=1)


def _head_major(t, n_q, slots):
    nb = t.shape[0] // n_q
    th = jnp.transpose(t.reshape(nb, n_q, N_HEADS, V_DIM), (0, 2, 1, 3))
    return jnp.pad(th, ((0, 0), (0, 0), (0, slots - n_q), (0, 0)))


def _pick_tile(n, pref):
    return pref if n % pref == 0 else n


def kernel(x_prompt, x_sample, mem_prompt, cache_k, cache_v, page_table, cache_mem_k, cache_mem_v, state_pool, rel_bias, norm_mix_g, w_in, lambda_q1, lambda_k1, lambda_q2, lambda_k2, subln_g, w_pool_grp, pool_scale, w_br_attn, w_br_pool, w_gate, w_out, norm_cross_g, w_cq, w_ck, w_cv, w_co, norm_ffn_g, w_router_grp, b_router_grp, w_router_exp, b_router_exp, w_exp_gate, w_exp_up, w_exp_down, norm_final_g):
    depth = w_in.shape[0]
    assert depth == 1, "single-layer step"
    l = 0
    b_p, s_p, d = x_prompt.shape
    b_s, s_s, _ = x_sample.shape
    n_p, n_s = b_p * s_p, b_s * s_s
    n_mem = mem_prompt.shape[1]
    page = cache_k.shape[2]
    past_len = page_table.shape[1] * page
    pool_w = state_pool.shape[-1]

    lam_init = 0.8 - 0.6 * math.exp(-0.3 * l)
    lam = (jnp.exp(jnp.sum(lambda_q1[l].astype(F32) * lambda_k1[l].astype(F32)))
           - jnp.exp(jnp.sum(lambda_q2[l].astype(F32) * lambda_k2[l].astype(F32))) + lam_init).reshape(1, 1)

    bf = lambda w: w.astype(BF16)
    w_in_bf, w_gate_bf, w_ba_bf, w_bp_bf, w_out_bf = bf(w_in[l]), bf(w_gate[l]), bf(w_br_attn[l]), bf(w_br_pool[l]), bf(w_out[l])
    w_cq_bf, w_ck_bf, w_cv_bf, w_co_bf = bf(w_cq[l]), bf(w_ck[l]), bf(w_cv[l]), bf(w_co[l])
    w_grp_bf = bf(w_pool_grp[l])
    w1_bf, w3_bf, w2_bf = bf(w_exp_gate[l]), bf(w_exp_up[l]), bf(w_exp_down[l])
    wr = jnp.concatenate([w_router_exp[l], w_router_grp[l]], axis=1).astype(F32)
    n_router = wr.shape[1]
    assert n_router <= LANES // 2
    wr_hi = wr.astype(BF16)
    wr_lo = (wr - wr_hi.astype(F32)).astype(BF16)
    lane_pad = lambda t, lo: jnp.pad(t, ((0, 0), (lo, LANES - lo - n_router)))
    wr_split = lane_pad(wr_hi, 0) + lane_pad(wr_lo, LANES // 2)
    br = jnp.concatenate([b_router_exp[l], b_router_grp[l]]).astype(F32)
    br = jnp.pad(br, (0, LANES - br.shape[0])).reshape(1, LANES)

    def token_tail(x_tok, a, pooled, mem_k, mem_v, nb, tile, cross_tile):
        x1 = _merge(x_tok, a, pooled, norm_mix_g[l], w_gate_bf, w_ba_bf, w_bp_bf, w_out_bf, tile)
        x1 = x1.reshape(nb, -1, d)
        seq = x1.shape[1]
        x1 = jnp.pad(x1, ((0, 0), (0, -seq % BF16_ROWS), (0, 0)))
        n_seq = CROSS_SEQS_PER_STEP if (seq <= BF16_ROWS and nb % CROSS_SEQS_PER_STEP == 0) else 1
        x2 = _cross(x1, mem_k, mem_v, norm_cross_g[l], w_cq_bf, w_co_bf, max(cross_tile, BF16_ROWS),
                    n_seq)[:, :seq]
        n_tok = x_tok.shape[0]
        return _moe_final(x2.reshape(-1, d), norm_ffn_g[l], wr_split, br, w1_bf, w3_bf, w2_bf,
                          norm_final_g, _pick_tile(n_tok, 1024), 2)

    tile_p = _pick_tile(n_p, 512)
    tq = _pick_tile(s_p, 256)
    tile_p = min(tile_p, s_p)
    assert s_p % tile_p == 0
    q, k, v, u, pooled = _inproj(x_prompt.reshape(n_p, d), norm_mix_g[l], w_in_bf, tile_p,
                                 pool=(s_p // tile_p, w_grp_bf, pool_scale[l]))
    bias_tiles = _prompt_bias_tiles(rel_bias, tq, N_NEAR + 1)
    a = _prompt_attention(lam, q.reshape(b_p, s_p, d), k.reshape(b_p, s_p, d), v.reshape(b_p, s_p, d),
                          bias_tiles, subln_g[l], lam_init, tq)
    u3 = u.reshape(b_p, s_p, pool_w)
    mem_k, mem_v = _memkv(mem_prompt.reshape(b_p * n_mem, d), w_ck_bf, w_cv_bf, _pick_tile(b_p * n_mem, 512))
    mem_k, mem_v = mem_k.reshape(b_p, n_mem, d), mem_v.reshape(b_p, n_mem, d)
    y_prompt = token_tail(x_prompt.reshape(n_p, d), a.reshape(n_p, d), pooled.reshape(n_p, pool_w),
                          mem_k, mem_v, b_p, tile_p, _pick_tile(s_p, 512)).reshape(b_p, s_p, d)
    k_prompt = k.reshape(1, b_p, s_p, N_HEADS, V_DIM)
    v_prompt = v.reshape(1, b_p, s_p, N_HEADS, V_DIM)
    pool_prompt = u3[:, -POOL_CTX:][None]
    x_heads_dim = d // X_HEADS
    mem_k_prompt = mem_k.reshape(1, b_p, n_mem, X_HEADS, x_heads_dim)
    mem_v_prompt = mem_v.reshape(1, b_p, n_mem, X_HEADS, x_heads_dim)

    qs, ks, vs, us = _inproj(x_sample.reshape(n_s, d), norm_mix_g[l], w_in_bf, n_s)
    slots = -(-s_s // 8) * 8
    bias_past, bias_new = _decode_bias(rel_bias, past_len, slots, s_s, page)
    paged = lambda c: c.reshape(-1, page * N_HEADS, V_DIM)
    a_s = _decode_attention(page_table.astype(jnp.int32), lam, _head_pair_queries(qs, s_s, slots),
                            paged(cache_k), paged(cache_v), bias_past,
                            _head_major(ks, s_s, slots), _head_major(vs, s_s, slots), bias_new,
                            subln_g[l], lam_init, page)[:, :s_s]
    ctx = state_pool[l].astype(F32)
    full = jnp.concatenate([ctx, us.reshape(b_s, s_s, pool_w)], axis=1)
    slot = full.shape[1] + (-full.shape[1] % (2 * BF16_ROWS))
    stream = jnp.pad(full, ((0, 0), (0, slot - full.shape[1]), (0, 0))).reshape(1, b_s * slot, pool_w)
    pooled_s = _pool_mix(stream, w_grp_bf, pool_scale[l], _pick_tile(b_s * slot, 256))
    pooled_s = pooled_s.reshape(b_s, slot, pool_w)[:, POOL_CTX:POOL_CTX + s_s]
    y_sample = token_tail(x_sample.reshape(n_s, d), a_s.reshape(n_s, d).astype(BF16), pooled_s.reshape(n_s, pool_w),
                          cache_mem_k[l].reshape(b_s, n_mem, d), cache_mem_v[l].reshape(b_s, n_mem, d),
                          b_s, n_s, s_s).reshape(b_s, s_s, d)
    k_sample = ks.reshape(1, b_s, s_s, N_HEADS, V_DIM)
    v_sample = vs.reshape(1, b_s, s_s, N_HEADS, V_DIM)
    pool_sample = full[:, -POOL_CTX:][None]

    return (y_prompt, y_sample, k_prompt, v_prompt, pool_prompt, mem_k_prompt, mem_v_prompt,
            k_sample, v_sample, pool_sample)
```
